```python
import jax, jax.numpy as jnp
from jax import lax
import numpy as np

D_MODEL = 1024
BATCH = 4
SEQ = 8192
DEPTH = 2

N_META = 16
MIX_WIDTH = D_MODEL
ATTN_WIDTH = MIX_WIDTH // 2
CONV_WIDTH = MIX_WIDTH - ATTN_WIDTH
HEAD_DIM = 64
N_Q_HEADS = ATTN_WIDTH // HEAD_DIM
N_KV_HEADS = 2
GROUP = N_Q_HEADS // N_KV_HEADS
KV_WIDTH = N_KV_HEADS * HEAD_DIM
CONV_GROUPS = 8
CONV_K = 3
WINDOW = 128
BLOCK = 128
LEAD_PAD = BLOCK - N_META
ROPE_THETA = 500000.0
ROT_DIM = HEAD_DIM // 4
D_FF = 4 * D_MODEL
IN_WIDTH = ATTN_WIDTH + 2 * KV_WIDTH + 3 * CONV_WIDTH
EPS = 1e-6

kernel_name = 'hymba_swa_sink_shortconv_sandwich'


def rmsnorm(x, g):
    x32 = x.astype(jnp.float32)
    y = x32 * lax.rsqrt(jnp.mean(x32 * x32, axis=-1, keepdims=True) + EPS)
    return y.astype(x.dtype) * g.astype(x.dtype)


def rope_tables(n_pos):
    pos = jnp.arange(n_pos, dtype=jnp.float32)
    inv_freq = jnp.power(jnp.float32(ROPE_THETA), -jnp.arange(0, ROT_DIM, 2, dtype=jnp.float32) / ROT_DIM)
    ang = pos[:, None] * inv_freq[None, :]
    return jnp.cos(ang), jnp.sin(ang)


def partial_rope(t, cos, sin):
    half = ROT_DIM // 2
    t32 = t[..., :ROT_DIM].astype(jnp.float32)
    t1, t2 = t32[..., :half], t32[..., half:]
    c, s = cos[None, :, None, :], sin[None, :, None, :]
    rot = jnp.concatenate([t1 * c - t2 * s, t2 * c + t1 * s], axis=-1).astype(t.dtype)
    return jnp.concatenate([rot, t[..., ROT_DIM:]], axis=-1)


def sliding_window_gqa_sinks(q, k, v, sink):
    bsz, L = q.shape[0], q.shape[1]
    pad = ((0, 0), (LEAD_PAD, 0), (0, 0), (0, 0))
    q, k, v = jnp.pad(q, pad), jnp.pad(k, pad), jnp.pad(v, pad)
    Lp = L + LEAD_PAD
    nb = Lp // BLOCK
    qb = q.reshape(bsz, nb, BLOCK, N_KV_HEADS, GROUP, HEAD_DIM)

    def band(t):
        tb = t.reshape(bsz, nb, BLOCK, N_KV_HEADS, HEAD_DIM)
        prev = jnp.pad(tb, ((0, 0), (1, 0), (0, 0), (0, 0), (0, 0)))[:, :-1]
        return jnp.concatenate([prev, tb], axis=2)

    kw, vw = band(k), band(v)
    s = jnp.einsum('bnqhgd,bnkhd->bnhgqk', qb, kw,
                   preferred_element_type=jnp.float32) * (HEAD_DIM ** -0.5)
    blk = jnp.arange(nb)[:, None, None]
    qpos = blk * BLOCK + jnp.arange(BLOCK)[None, :, None]
    kpos = (blk - 1) * BLOCK + jnp.arange(2 * BLOCK)[None, None, :]
    mask = (kpos <= qpos) & (qpos - kpos < WINDOW) & (kpos >= LEAD_PAD)
    s = jnp.where(mask[None, :, None, None], s, -jnp.inf)
    sk = sink.astype(jnp.float32).reshape(1, 1, N_KV_HEADS, GROUP, 1, 1)
    m = jnp.maximum(jnp.max(s, axis=-1, keepdims=True), sk)
    e = jnp.exp(s - m)
    p = e / (jnp.sum(e, axis=-1, keepdims=True) + jnp.exp(sk - m))
    o = jnp.einsum('bnhgqk,bnkhd->bnqhgd', p.astype(v.dtype), vw)
    return o.reshape(bsz, Lp, N_Q_HEADS * HEAD_DIM)[:, LEAD_PAD:]


def short_gated_conv(b_gate, c_gate, h, w):
    u = c_gate * h
    y = lax.conv_general_dilated(u, w[:, None, :].astype(u.dtype), window_strides=(1,),
                                 padding=[(CONV_K - 1, 0)],
                                 dimension_numbers=('NWC', 'WIO', 'NWC'),
                                 feature_group_count=CONV_WIDTH)
    return b_gate * y


def setup_inputs(seed: int = 0) -> dict:
    key = jax.random.key(seed)
    ks = jax.random.split(key, 16)
    f32 = jnp.float32

    def nrm(k, shape, scale):
        return jax.random.normal(k, shape, f32) * scale

    def gain(k, shape):
        return 1.0 + 0.05 * jax.random.normal(k, shape, f32)

    return {
        'x': nrm(ks[0], (BATCH, SEQ, D_MODEL), 1.0),
        'meta_tokens': nrm(ks[1], (N_META, D_MODEL), 1.0),
        'mix_pre_g': gain(ks[2], (DEPTH, D_MODEL)),
        'w_in': nrm(ks[3], (DEPTH, D_MODEL, IN_WIDTH), D_MODEL ** -0.5),
        'conv_w': nrm(ks[4], (DEPTH, CONV_K, CONV_WIDTH), CONV_K ** -0.5),
        'sinks': nrm(ks[5], (DEPTH, N_Q_HEADS), 0.5),
        'attn_out_g': gain(ks[6], (DEPTH, ATTN_WIDTH)),
        'conv_out_g': gain(ks[7], (DEPTH, CONV_WIDTH)),
        'w_out': nrm(ks[8], (DEPTH, MIX_WIDTH, D_MODEL), MIX_WIDTH ** -0.5),
        'mix_post_g': gain(ks[9], (DEPTH, D_MODEL)),
        'mlp_pre_g': gain(ks[10], (DEPTH, D_MODEL)),
        'w_up': nrm(ks[11], (DEPTH, D_MODEL, D_FF), D_MODEL ** -0.5),
        'w_down': nrm(ks[12], (DEPTH, D_FF, D_MODEL), D_FF ** -0.5),
        'mlp_post_g': gain(ks[13], (DEPTH, D_MODEL)),
    }


def reference(x, meta_tokens, mix_pre_g, w_in, conv_w, sinks, attn_out_g, conv_out_g,
              w_out, mix_post_g, mlp_pre_g, w_up, w_down, mlp_post_g):
    bsz = x.shape[0]
    meta = jnp.broadcast_to(meta_tokens[None].astype(x.dtype), (bsz, N_META, D_MODEL))
    h = jnp.concatenate([meta, x], axis=1)
    L = h.shape[1]
    cos, sin = rope_tables(L)
    s_q = ATTN_WIDTH
    s_k = s_q + KV_WIDTH
    s_v = s_k + KV_WIDTH
    s_b = s_v + CONV_WIDTH
    s_c = s_b + CONV_WIDTH
    for l in range(DEPTH):
        a = rmsnorm(h, mix_pre_g[l])
        proj = a @ w_in[l]
        q = proj[..., :s_q].reshape(bsz, L, N_Q_HEADS, HEAD_DIM)
        k = proj[..., s_q:s_k].reshape(bsz, L, N_KV_HEADS, HEAD_DIM)
        v = proj[..., s_k:s_v].reshape(bsz, L, N_KV_HEADS, HEAD_DIM)
        b_gate = proj[..., s_v:s_b]
        c_gate = proj[..., s_b:s_c]
        hc = proj[..., s_c:]
        q = partial_rope(q, cos, sin)
        k = partial_rope(k, cos, sin)
        y_attn = sliding_window_gqa_sinks(q, k, v, sinks[l])
        y_conv = short_gated_conv(b_gate, c_gate, hc, conv_w[l])
        y = jnp.concatenate([rmsnorm(y_attn, attn_out_g[l]),
                             rmsnorm(y_conv, conv_out_g[l])], axis=-1)
        h = h + rmsnorm(y @ w_out[l], mix_post_g[l])
        a = rmsnorm(h, mlp_pre_g[l])
        f = jnp.square(jax.nn.relu(a @ w_up[l])) @ w_down[l]
        h = h + rmsnorm(f, mlp_post_g[l])
    return h[:, N_META:]
```

```python
import functools

import jax
import jax.numpy as jnp
from jax import lax
from jax.experimental import pallas as pl
from jax.experimental.pallas import tpu as pltpu

D_MODEL = 1024
N_META = 16
ATTN_WIDTH = 512
CONV_WIDTH = 512
HEAD_DIM = 64
N_Q_HEADS = 8
N_KV_HEADS = 2
GROUP = N_Q_HEADS // N_KV_HEADS
KV_WIDTH = N_KV_HEADS * HEAD_DIM
CONV_K = 3
WINDOW = 128
ROPE_THETA = 500000.0
ROT_DIM = HEAD_DIM // 4
D_FF = 4 * D_MODEL
IN_WIDTH = ATTN_WIDTH + 2 * KV_WIDTH + 3 * CONV_WIDTH
EPS = 1e-6

S_Q = ATTN_WIDTH
S_K = S_Q + KV_WIDTH
S_V = S_K + KV_WIDTH
S_B = S_V + CONV_WIDTH
S_C = S_B + CONV_WIDTH

LANES = 128
CONV_CARRY_ROWS = 8
MASKED = -1e30
ROW_TILE = 512
FF_CHUNK = 1024
VMEM_LIMIT_BYTES = 56 * 1024 * 1024


def _rms(x):
    return x * lax.rsqrt(jnp.mean(x * x, axis=-1, keepdims=True) + EPS)


def _rope(t, tab):
    c = tab[:, 0:LANES]
    s_lo = tab[:, LANES:2 * LANES]
    s_hi = tab[:, 2 * LANES:3 * LANES]
    half = ROT_DIM // 2
    partner_hi = pltpu.roll(t, LANES - half, axis=1)
    partner_lo = pltpu.roll(t, half, axis=1)
    return t * c + partner_hi * s_lo + partner_lo * s_hi


def _mixer_kernel(sinks_ref, h_ref, tab_ref, ck_ref, cv_ref, cu_ref, cb_ref,
                  pre_g_ref, w_in_ref, conv_w_ref, attn_g_ref, conv_g_ref, w_out_ref, post_g_ref,
                  o_ref, ok_ref, ov_ref, ou_ref,
                  q_buf, k_buf, v_buf, u_buf, ya_buf, y_buf, *, tm, qb):
    t = pl.program_id(1)
    kb = WINDOW + qb
    n_qb = tm // qb

    @pl.when(t == 0)
    def _():
        k_buf[0:WINDOW, :] = ck_ref[...]
        v_buf[0:WINDOW, :] = cv_ref[...]
        u_buf[0:CONV_CARRY_ROWS, :] = cu_ref[...]

    h = h_ref[...]
    a = (_rms(h) * pre_g_ref[...]).astype(jnp.bfloat16)
    proj = jnp.dot(a, w_in_ref[...], preferred_element_type=jnp.float32)

    tab = tab_ref[...]
    for c in range(ATTN_WIDTH // LANES):
        qc = _rope(proj[:, c * LANES:(c + 1) * LANES], tab) * (HEAD_DIM ** -0.5)
        q_buf[:, c * LANES:(c + 1) * LANES] = qc.astype(jnp.bfloat16)
    k_buf[WINDOW:WINDOW + tm, :] = _rope(proj[:, S_Q:S_K], tab).astype(jnp.bfloat16)
    v_buf[WINDOW:WINDOW + tm, :] = proj[:, S_K:S_V].astype(jnp.bfloat16)

    u = proj[:, S_B:S_C] * proj[:, S_C:]
    u_buf[CONV_CARRY_ROWS:CONV_CARRY_ROWS + tm, :] = u
    cw = conv_w_ref[...]
    y_conv = cw[2:3, :] * u
    y_conv += cw[1:2, :] * u_buf[CONV_CARRY_ROWS - 1:CONV_CARRY_ROWS - 1 + tm, :]
    y_conv += cw[0:1, :] * u_buf[CONV_CARRY_ROWS - 2:CONV_CARRY_ROWS - 2 + tm, :]
    y_conv = proj[:, S_V:S_B] * y_conv
    y_buf[:, ATTN_WIDTH:] = (_rms(y_conv) * conv_g_ref[...]).astype(jnp.bfloat16)

    r_idx = lax.broadcasted_iota(jnp.int32, (qb, kb), 0)
    c_idx = lax.broadcasted_iota(jnp.int32, (qb, kb), 1)
    band = jnp.where((c_idx > r_idx) & (c_idx <= r_idx + WINDOW), 0.0, MASKED)
    carry_bias = jnp.concatenate([cb_ref[...], jnp.zeros((1, qb), jnp.float32)], axis=1)
    first_tile = (t == 0).astype(jnp.float32)

    def attn_block(j, carry):
        r0 = pl.multiple_of(j * qb, qb)
        first = jnp.where(j == 0, first_tile, 0.0)
        bias = band + carry_bias * first
        kk = k_buf[pl.ds(r0, kb), :]
        vv = v_buf[pl.ds(r0, kb), :]
        for hd in range(N_Q_HEADS):
            g = hd // GROUP
            qh = q_buf[pl.ds(r0, qb), hd * HEAD_DIM:(hd + 1) * HEAD_DIM]
            s = lax.dot_general(qh, kk[:, g * HEAD_DIM:(g + 1) * HEAD_DIM],
                                (((1,), (1,)), ((), ())),
                                preferred_element_type=jnp.float32) + bias
            sink = sinks_ref[hd]
            m = jnp.maximum(jnp.max(s, axis=-1, keepdims=True), sink)
            e = jnp.exp(s - m)
            den = jnp.sum(e, axis=-1, keepdims=True) + jnp.exp(sink - m)
            o = jnp.dot(e.astype(jnp.bfloat16), vv[:, g * HEAD_DIM:(g + 1) * HEAD_DIM],
                        preferred_element_type=jnp.float32)
            ya_buf[pl.ds(r0, qb), hd * HEAD_DIM:(hd + 1) * HEAD_DIM] = o / den
        return carry

    if n_qb == 1:
        attn_block(0, 0)
    else:
        lax.fori_loop(0, n_qb, attn_block, 0)

    y_buf[:, 0:ATTN_WIDTH] = (_rms(ya_buf[...]) * attn_g_ref[...]).astype(jnp.bfloat16)
    mixed = jnp.dot(y_buf[...], w_out_ref[...], preferred_element_type=jnp.float32)
    o_ref[...] = h + _rms(mixed) * post_g_ref[...]

    k_tail = k_buf[tm:tm + WINDOW, :]
    v_tail = v_buf[tm:tm + WINDOW, :]
    u_tail = u_buf[tm:tm + CONV_CARRY_ROWS, :]
    k_buf[0:WINDOW, :] = k_tail
    v_buf[0:WINDOW, :] = v_tail
    u_buf[0:CONV_CARRY_ROWS, :] = u_tail
    ok_ref[0] = k_tail
    ov_ref[0] = v_tail
    ou_ref[0] = u_tail


def _mixer(h, tab, ck, cv, cu, cb, sinks, pre_g, w_in, conv_w, attn_g, conv_g, w_out, post_g,
           *, batch, tm):
    rows = h.shape[0]
    seq = rows // batch
    n_tiles = seq // tm
    qb = min(WINDOW, tm)
    const = lambda b, t: (0, 0)
    row_map = lambda b, t: (b * n_tiles + t, 0)
    kernel = functools.partial(_mixer_kernel, tm=tm, qb=qb)
    return pl.pallas_call(
        kernel,
        grid=(batch, n_tiles),
        in_specs=[
            pl.BlockSpec(memory_space=pltpu.SMEM),
            pl.BlockSpec((tm, D_MODEL), row_map),
            pl.BlockSpec((tm, 3 * LANES), lambda b, t: (t, 0)),
            pl.BlockSpec((WINDOW, KV_WIDTH), const),
            pl.BlockSpec((WINDOW, KV_WIDTH), const),
            pl.BlockSpec((CONV_CARRY_ROWS, CONV_WIDTH), const),
            pl.BlockSpec((1, WINDOW), const),
            pl.BlockSpec((1, D_MODEL), const),
            pl.BlockSpec((D_MODEL, IN_WIDTH), const),
            pl.BlockSpec((CONV_K, CONV_WIDTH), const),
            pl.BlockSpec((1, ATTN_WIDTH), const),
            pl.BlockSpec((1, CONV_WIDTH), const),
            pl.BlockSpec((D_MODEL, D_MODEL), const),
            pl.BlockSpec((1, D_MODEL), const),
        ],
        out_specs=[
            pl.BlockSpec((tm, D_MODEL), row_map),
            pl.BlockSpec((1, WINDOW, KV_WIDTH), lambda b, t: (b, 0, 0)),
            pl.BlockSpec((1, WINDOW, KV_WIDTH), lambda b, t: (b, 0, 0)),
            pl.BlockSpec((1, CONV_CARRY_ROWS, CONV_WIDTH), lambda b, t: (b, 0, 0)),
        ],
        out_shape=[
            jax.ShapeDtypeStruct((rows, D_MODEL), jnp.float32),
            jax.ShapeDtypeStruct((batch, WINDOW, KV_WIDTH), jnp.bfloat16),
            jax.ShapeDtypeStruct((batch, WINDOW, KV_WIDTH), jnp.bfloat16),
            jax.ShapeDtypeStruct((batch, CONV_CARRY_ROWS, CONV_WIDTH), jnp.float32),
        ],
        scratch_shapes=[
            pltpu.VMEM((tm, ATTN_WIDTH), jnp.bfloat16),
            pltpu.VMEM((WINDOW + tm, KV_WIDTH), jnp.bfloat16),
            pltpu.VMEM((WINDOW + tm, KV_WIDTH), jnp.bfloat16),
            pltpu.VMEM((CONV_CARRY_ROWS + tm, CONV_WIDTH), jnp.float32),
            pltpu.VMEM((tm, ATTN_WIDTH), jnp.float32),
            pltpu.VMEM((tm, ATTN_WIDTH + CONV_WIDTH), jnp.bfloat16),
        ],
        compiler_params=pltpu.CompilerParams(
            dimension_semantics=("arbitrary", "arbitrary"),
            vmem_limit_bytes=VMEM_LIMIT_BYTES),
        name="mixer",
    )(sinks, h, tab, ck, cv, cu, cb, pre_g, w_in, conv_w, attn_g, conv_g, w_out, post_g)


def _mlp_kernel(h_ref, pre_g_ref, w_up_ref, w_down_ref, post_g_ref, o_ref):
    h = h_ref[...]
    a = (_rms(h) * pre_g_ref[...]).astype(jnp.bfloat16)
    acc = jnp.zeros(h.shape, jnp.float32)
    for c in range(D_FF // FF_CHUNK):
        cols = slice(c * FF_CHUNK, (c + 1) * FF_CHUNK)
        up = jnp.dot(a, w_up_ref[:, cols], preferred_element_type=jnp.float32)
        act = jnp.square(jnp.maximum(up, 0.0)).astype(jnp.bfloat16)
        acc = acc + jnp.dot(act, w_down_ref[cols, :], preferred_element_type=jnp.float32)
    o_ref[...] = h + _rms(acc) * post_g_ref[...]


def _mlp(h, pre_g, w_up, w_down, post_g, *, tm):
    rows = h.shape[0]
    const = lambda i: (0, 0)
    return pl.pallas_call(
        _mlp_kernel,
        grid=(rows // tm,),
        in_specs=[
            pl.BlockSpec((tm, D_MODEL), lambda i: (i, 0)),
            pl.BlockSpec((1, D_MODEL), const),
            pl.BlockSpec((D_MODEL, D_FF), const, pipeline_mode=pl.Buffered(1)),
            pl.BlockSpec((D_FF, D_MODEL), const, pipeline_mode=pl.Buffered(1)),
            pl.BlockSpec((1, D_MODEL), const),
        ],
        out_specs=pl.BlockSpec((tm, D_MODEL), lambda i: (i, 0)),
        out_shape=jax.ShapeDtypeStruct((rows, D_MODEL), jnp.float32),
        compiler_params=pltpu.CompilerParams(
            dimension_semantics=("arbitrary",),
            vmem_limit_bytes=VMEM_LIMIT_BYTES),
        name="mlp",
    )(h, pre_g, w_up, w_down, post_g)


def _rope_table(n_pos):
    pos = jnp.arange(n_pos, dtype=jnp.float32)
    inv_freq = jnp.power(jnp.float32(ROPE_THETA),
                         -jnp.arange(0, ROT_DIM, 2, dtype=jnp.float32) / ROT_DIM)
    ang = pos[:, None] * inv_freq[None, :]
    cos, sin = jnp.cos(ang), jnp.sin(ang)
    half = ROT_DIM // 2
    d = jnp.arange(LANES) % HEAD_DIM
    idx = d % half
    cos_l, sin_l = cos[:, idx], sin[:, idx]
    c = jnp.where(d < ROT_DIM, cos_l, 1.0)
    s_lo = jnp.where(d < half, -sin_l, 0.0)
    s_hi = jnp.where((d >= half) & (d < ROT_DIM), sin_l, 0.0)
    return jnp.concatenate([c, s_lo, s_hi], axis=1)


def kernel(x, meta_tokens, mix_pre_g, w_in, conv_w, sinks, attn_out_g, conv_out_g, w_out,
           mix_post_g, mlp_pre_g, w_up, w_down, mlp_post_g):
    batch, seq, _ = x.shape
    depth = w_in.shape[0]
    bf16 = jnp.bfloat16

    tab = _rope_table(N_META + seq)
    tab_meta, tab_x = tab[:N_META], tab[N_META:]

    hx = x.reshape(batch * seq, D_MODEL)
    hm = meta_tokens.astype(x.dtype)

    zero_kv = jnp.zeros((WINDOW, KV_WIDTH), bf16)
    zero_u = jnp.zeros((CONV_CARRY_ROWS, CONV_WIDTH), jnp.float32)
    none_valid = jnp.full((1, WINDOW), MASKED, jnp.float32)
    meta_valid = jnp.where(jnp.arange(WINDOW)[None, :] >= WINDOW - N_META, 0.0, MASKED
                           ).astype(jnp.float32)

    row = lambda v: v.reshape(1, -1)
    for l in range(depth):
        mixer_w = (sinks[l], row(mix_pre_g[l]), w_in[l].astype(bf16), conv_w[l],
                   row(attn_out_g[l]), row(conv_out_g[l]), w_out[l].astype(bf16),
                   row(mix_post_g[l]))
        mlp_w = (row(mlp_pre_g[l]), w_up[l].astype(bf16), w_down[l].astype(bf16),
                 row(mlp_post_g[l]))
        hm, mk, mv, mu = _mixer(hm, tab_meta, zero_kv, zero_kv, zero_u, none_valid, *mixer_w,
                                batch=1, tm=N_META)
        hx = _mixer(hx, tab_x, mk[0], mv[0], mu[0], meta_valid, *mixer_w,
                    batch=batch, tm=ROW_TILE)[0]
        if l + 1 < depth:
            hm = _mlp(hm, *mlp_w, tm=N_META)
        hx = _mlp(hx, *mlp_w, tm=ROW_TILE)
    return hx.reshape(batch, seq, D_MODEL)
```

```python
import functools

import jax
import jax.numpy as jnp
from jax import lax
from jax.experimental import pallas as pl
from jax.experimental.pallas import tpu as pltpu

D_MODEL = 1024
N_META = 16
ATTN_WIDTH = 512
CONV_WIDTH = 512
HEAD_DIM = 64
N_Q_HEADS = 8
N_KV_HEADS = 2
GROUP = N_Q_HEADS // N_KV_HEADS
KV_WIDTH = N_KV_HEADS * HEAD_DIM
CONV_K = 3
WINDOW = 128
ROPE_THETA = 500000.0
ROT_DIM = HEAD_DIM // 4
D_FF = 4 * D_MODEL
IN_WIDTH = ATTN_WIDTH + 2 * KV_WIDTH + 3 * CONV_WIDTH
EPS = 1e-6

S_Q = ATTN_WIDTH
S_K = S_Q + KV_WIDTH
S_V = S_K + KV_WIDTH
S_B = S_V + CONV_WIDTH
S_C = S_B + CONV_WIDTH

LANES = 128
QB = WINDOW
KB = WINDOW + QB
CONV_CARRY_ROWS = 8
MASKED = -1e30
ROW_TILE = 512
FF_CHUNK = 1024
VMEM_LIMIT_BYTES = 56 * 1024 * 1024


def _rms(x):
    return x * lax.rsqrt(jnp.mean(x * x, axis=-1, keepdims=True) + EPS)


def _rope(t, tab):
    c = tab[:, 0:LANES]
    s_lo = tab[:, LANES:2 * LANES]
    s_hi = tab[:, 2 * LANES:3 * LANES]
    half = ROT_DIM // 2
    partner_hi = pltpu.roll(t, LANES - half, axis=1)
    partner_lo = pltpu.roll(t, half, axis=1)
    return t * c + partner_hi * s_lo + partner_lo * s_hi


def _mixer_kernel(sinks_ref, h_ref, tab_ref, ck_ref, cv_ref, cu_ref, cb_ref,
                  pre_g_ref, w_in_ref, conv_w_ref, attn_g_ref, conv_g_ref, w_out_ref, post_g_ref,
                  o_ref, ok_ref, ov_ref, ou_ref,
                  q_buf, k_buf, vt_buf, u_buf, yat_buf, yc_buf, bias_buf, *, tm):
    t = pl.program_id(1)
    n_qb = tm // QB

    @pl.when(t == 0)
    def _():
        k_buf[:, 0:WINDOW, :] = ck_ref[...]
        vt_buf[:, 0:WINDOW] = cv_ref[...]
        u_buf[0:CONV_CARRY_ROWS, :] = cu_ref[...]

    key_idx = lax.broadcasted_iota(jnp.int32, (KB, QB), 0)
    qry_idx = lax.broadcasted_iota(jnp.int32, (KB, QB), 1)
    band = jnp.where((key_idx > qry_idx) & (key_idx <= qry_idx + WINDOW), 0.0, MASKED)
    band = jnp.concatenate([band] * GROUP, axis=1)
    bias_buf[1] = band
    bias_buf[0] = band + jnp.where(t == 0, cb_ref[...], 0.0)

    h = h_ref[...]
    a = (_rms(h) * pre_g_ref[...]).astype(jnp.bfloat16)
    proj = jnp.dot(a, w_in_ref[...], preferred_element_type=jnp.float32)

    tab = tab_ref[...]
    for c in range(ATTN_WIDTH // LANES):
        qc = (_rope(proj[:, c * LANES:(c + 1) * LANES], tab) * (HEAD_DIM ** -0.5)
              ).astype(jnp.bfloat16)
        q_buf[2 * c] = qc[:, :HEAD_DIM]
        q_buf[2 * c + 1] = qc[:, HEAD_DIM:]
    kr = _rope(proj[:, S_Q:S_K], tab).astype(jnp.bfloat16)
    for g in range(N_KV_HEADS):
        k_buf[g, WINDOW:WINDOW + tm, :] = kr[:, g * HEAD_DIM:(g + 1) * HEAD_DIM]
    vt_buf[:, WINDOW:WINDOW + tm] = proj[:, S_K:S_V].T

    u = proj[:, S_B:S_C] * proj[:, S_C:]
    u_buf[CONV_CARRY_ROWS:CONV_CARRY_ROWS + tm, :] = u
    cw = conv_w_ref[...]
    y_conv = cw[2:3, :] * u
    y_conv += cw[1:2, :] * u_buf[CONV_CARRY_ROWS - 1:CONV_CARRY_ROWS - 1 + tm, :]
    y_conv += cw[0:1, :] * u_buf[CONV_CARRY_ROWS - 2:CONV_CARRY_ROWS - 2 + tm, :]
    y_conv = proj[:, S_V:S_B] * y_conv
    yc_buf[...] = (_rms(y_conv) * conv_g_ref[...]).astype(jnp.bfloat16)

    for j in range(n_qb):
        r0 = j * QB
        bias = bias_buf[min(j, 1)]
        for g in range(N_KV_HEADS):
            kg = k_buf[g, r0:r0 + KB, :]
            qg = q_buf[g * GROUP:(g + 1) * GROUP, r0:r0 + QB, :].reshape(GROUP * QB, HEAD_DIM)
            s = lax.dot_general(kg, qg, (((1,), (1,)), ((), ())),
                                preferred_element_type=jnp.float32) + bias
            sink = jnp.concatenate(
                [jnp.full((1, QB), sinks_ref[g * GROUP + i], jnp.float32) for i in range(GROUP)],
                axis=1)
            m = jnp.maximum(jnp.max(s, axis=0, keepdims=True), sink)
            e = jnp.exp(s - m)
            den = jnp.sum(e, axis=0, keepdims=True) + jnp.exp(sink - m)
            vt = vt_buf[g * HEAD_DIM:(g + 1) * HEAD_DIM, r0:r0 + KB].astype(jnp.bfloat16)
            o = jnp.dot(vt, e.astype(jnp.bfloat16), preferred_element_type=jnp.float32)
            o = o * (1.0 / den)
            for i in range(GROUP):
                hd = g * GROUP + i
                yat_buf[hd * HEAD_DIM:(hd + 1) * HEAD_DIM, r0:r0 + QB] = o[:, i * QB:(i + 1) * QB]

    yat = yat_buf[...]
    inv = lax.rsqrt(jnp.mean(yat * yat, axis=0, keepdims=True) + EPS)
    ya_n = (yat * inv * attn_g_ref[...]).astype(jnp.bfloat16)
    mixed = lax.dot_general(ya_n, w_out_ref[0:ATTN_WIDTH, :], (((0,), (0,)), ((), ())),
                            preferred_element_type=jnp.float32)
    mixed += jnp.dot(yc_buf[...], w_out_ref[ATTN_WIDTH:, :], preferred_element_type=jnp.float32)
    o_ref[...] = h + _rms(mixed) * post_g_ref[...]

    k_tail = k_buf[:, tm:tm + WINDOW, :]
    vt_tail = vt_buf[:, tm:tm + WINDOW]
    u_tail = u_buf[tm:tm + CONV_CARRY_ROWS, :]
    k_buf[:, 0:WINDOW, :] = k_tail
    vt_buf[:, 0:WINDOW] = vt_tail
    u_buf[0:CONV_CARRY_ROWS, :] = u_tail
    ok_ref[0] = k_tail
    ov_ref[0] = vt_tail
    ou_ref[0] = u_tail


def _mixer(h, tab, ck, cv, cu, cb, sinks, pre_g, w_in, conv_w, attn_g, conv_g, w_out, post_g,
           *, batch, tm):
    rows = h.shape[0]
    seq = rows // batch
    n_tiles = seq // tm
    const = lambda b, t: (0, 0)
    const3 = lambda b, t: (0, 0, 0)
    row_map = lambda b, t: (b * n_tiles + t, 0)
    kernel = functools.partial(_mixer_kernel, tm=tm)
    return pl.pallas_call(
        kernel,
        grid=(batch, n_tiles),
        in_specs=[
            pl.BlockSpec(memory_space=pltpu.SMEM),
            pl.BlockSpec((tm, D_MODEL), row_map),
            pl.BlockSpec((tm, 3 * LANES), lambda b, t: (t, 0)),
            pl.BlockSpec((N_KV_HEADS, WINDOW, HEAD_DIM), const3),
            pl.BlockSpec((KV_WIDTH, WINDOW), const),
            pl.BlockSpec((CONV_CARRY_ROWS, CONV_WIDTH), const),
            pl.BlockSpec((KB, 1), const),
            pl.BlockSpec((1, D_MODEL), const),
            pl.BlockSpec((D_MODEL, IN_WIDTH), const),
            pl.BlockSpec((CONV_K, CONV_WIDTH), const),
            pl.BlockSpec((ATTN_WIDTH, 1), const),
            pl.BlockSpec((1, CONV_WIDTH), const),
            pl.BlockSpec((D_MODEL, D_MODEL), const),
            pl.BlockSpec((1, D_MODEL), const),
        ],
        out_specs=[
            pl.BlockSpec((tm, D_MODEL), row_map),
            pl.BlockSpec((1, N_KV_HEADS, WINDOW, HEAD_DIM), lambda b, t: (b, 0, 0, 0)),
            pl.BlockSpec((1, KV_WIDTH, WINDOW), lambda b, t: (b, 0, 0)),
            pl.BlockSpec((1, CONV_CARRY_ROWS, CONV_WIDTH), lambda b, t: (b, 0, 0)),
        ],
        out_shape=[
            jax.ShapeDtypeStruct((rows, D_MODEL), jnp.float32),
            jax.ShapeDtypeStruct((batch, N_KV_HEADS, WINDOW, HEAD_DIM), jnp.bfloat16),
            jax.ShapeDtypeStruct((batch, KV_WIDTH, WINDOW), jnp.float32),
            jax.ShapeDtypeStruct((batch, CONV_CARRY_ROWS, CONV_WIDTH), jnp.float32),
        ],
        scratch_shapes=[
            pltpu.VMEM((N_Q_HEADS, tm, HEAD_DIM), jnp.bfloat16),
            pltpu.VMEM((N_KV_HEADS, WINDOW + tm, HEAD_DIM), jnp.bfloat16),
            pltpu.VMEM((KV_WIDTH, WINDOW + tm), jnp.float32),
            pltpu.VMEM((CONV_CARRY_ROWS + tm, CONV_WIDTH), jnp.float32),
            pltpu.VMEM((ATTN_WIDTH, tm), jnp.float32),
            pltpu.VMEM((tm, CONV_WIDTH), jnp.bfloat16),
            pltpu.VMEM((2, KB, GROUP * QB), jnp.float32),
        ],
        compiler_params=pltpu.CompilerParams(
            dimension_semantics=("arbitrary", "arbitrary"),
            vmem_limit_bytes=VMEM_LIMIT_BYTES),
        name="mixer",
    )(sinks, h, tab, ck, cv, cu, cb, pre_g, w_in, conv_w, attn_g, conv_g, w_out, post_g)


def _mlp_kernel(h_ref, pre_g_ref, w_up_ref, w_down_ref, post_g_ref, o_ref):
    h = h_ref[...]
    a = (_rms(h) * pre_g_ref[...]).astype(jnp.bfloat16)
    acc = jnp.zeros(h.shape, jnp.float32)
    for c in range(D_FF // FF_CHUNK):
        cols = slice(c * FF_CHUNK, (c + 1) * FF_CHUNK)
        up = jnp.dot(a, w_up_ref[:, cols], preferred_element_type=jnp.float32)
        act = jnp.square(jnp.maximum(up, 0.0)).astype(jnp.bfloat16)
        acc = acc + jnp.dot(act, w_down_ref[cols, :], preferred_element_type=jnp.float32)
    o_ref[...] = h + _rms(acc) * post_g_ref[...]


def _mlp(h, pre_g, w_up, w_down, post_g, *, tm):
    rows = h.shape[0]
    const = lambda i: (0, 0)
    return pl.pallas_call(
        _mlp_kernel,
        grid=(rows // tm,),
        in_specs=[
            pl.BlockSpec((tm, D_MODEL), lambda i: (i, 0)),
            pl.BlockSpec((1, D_MODEL), const),
            pl.BlockSpec((D_MODEL, D_FF), const, pipeline_mode=pl.Buffered(1)),
            pl.BlockSpec((D_FF, D_MODEL), const, pipeline_mode=pl.Buffered(1)),
            pl.BlockSpec((1, D_MODEL), const),
        ],
        out_specs=pl.BlockSpec((tm, D_MODEL), lambda i: (i, 0)),
        out_shape=jax.ShapeDtypeStruct((rows, D_MODEL), jnp.float32),
        compiler_params=pltpu.CompilerParams(
            dimension_semantics=("arbitrary",),
            vmem_limit_bytes=VMEM_LIMIT_BYTES),
        name="mlp",
    )(h, pre_g, w_up, w_down, post_g)


def _rope_table(n_pos):
    pos = jnp.arange(n_pos, dtype=jnp.float32)
    inv_freq = jnp.power(jnp.float32(ROPE_THETA),
                         -jnp.arange(0, ROT_DIM, 2, dtype=jnp.float32) / ROT_DIM)
    ang = pos[:, None] * inv_freq[None, :]
    cos, sin = jnp.cos(ang), jnp.sin(ang)
    half = ROT_DIM // 2
    d = jnp.arange(LANES) % HEAD_DIM
    idx = d % half
    cos_l, sin_l = cos[:, idx], sin[:, idx]
    c = jnp.where(d < ROT_DIM, cos_l, 1.0)
    s_lo = jnp.where(d < half, -sin_l, 0.0)
    s_hi = jnp.where((d >= half) & (d < ROT_DIM), sin_l, 0.0)
    return jnp.concatenate([c, s_lo, s_hi], axis=1)


def kernel(x, meta_tokens, mix_pre_g, w_in, conv_w, sinks, attn_out_g, conv_out_g, w_out,
           mix_post_g, mlp_pre_g, w_up, w_down, mlp_post_g):
    batch, seq, _ = x.shape
    depth = w_in.shape[0]
    bf16 = jnp.bfloat16
    f32 = jnp.float32
    lead = QB - N_META

    tab = _rope_table(N_META + seq)
    tab_meta = jnp.pad(tab[:N_META], ((lead, 0), (0, 0)))
    tab_x = tab[N_META:]

    hx = x.reshape(batch * seq, D_MODEL)
    hm = jnp.pad(meta_tokens.astype(x.dtype), ((lead, 0), (0, 0)))

    masked = jnp.full((WINDOW, 1), MASKED, f32)
    tail_valid = jnp.where(jnp.arange(WINDOW)[:, None] >= lead, 0.0, MASKED).astype(f32)
    cb_meta = jnp.concatenate([masked, tail_valid], axis=0)
    cb_x = jnp.concatenate([tail_valid, jnp.zeros((QB, 1), f32)], axis=0)
    zero_k = jnp.zeros((N_KV_HEADS, WINDOW, HEAD_DIM), bf16)
    zero_vt = jnp.zeros((KV_WIDTH, WINDOW), f32)
    zero_u = jnp.zeros((CONV_CARRY_ROWS, CONV_WIDTH), f32)

    row = lambda v: v.reshape(1, -1)
    for l in range(depth):
        mixer_w = (sinks[l], row(mix_pre_g[l]), w_in[l].astype(bf16), conv_w[l],
                   attn_out_g[l].reshape(-1, 1), row(conv_out_g[l]), w_out[l].astype(bf16),
                   row(mix_post_g[l]))
        mlp_w = (row(mlp_pre_g[l]), w_up[l].astype(bf16), w_down[l].astype(bf16),
                 row(mlp_post_g[l]))
        hm, mk, mv, mu = _mixer(hm, tab_meta, zero_k, zero_vt, zero_u, cb_meta, *mixer_w,
                                batch=1, tm=QB)
        hx = _mixer(hx, tab_x, mk[0], mv[0], mu[0], cb_x, *mixer_w,
                    batch=batch, tm=ROW_TILE)[0]
        if l + 1 < depth:
            hm = _mlp(hm, *mlp_w, tm=QB)
        hx = _mlp(hx, *mlp_w, tm=ROW_TILE)
    return hx.reshape(batch, seq, D_MODEL)
```

```python
import functools

import jax
import jax.numpy as jnp
from jax import lax
from jax.experimental import pallas as pl
from jax.experimental.pallas import tpu as pltpu

D_MODEL = 1024
N_META = 16
ATTN_WIDTH = 512
CONV_WIDTH = 512
HEAD_DIM = 64
N_Q_HEADS = 8
N_KV_HEADS = 2
GROUP = N_Q_HEADS // N_KV_HEADS
KV_WIDTH = N_KV_HEADS * HEAD_DIM
CONV_K = 3
WINDOW = 128
ROPE_THETA = 500000.0
ROT_DIM = HEAD_DIM // 4
D_FF = 4 * D_MODEL
IN_WIDTH = ATTN_WIDTH + 2 * KV_WIDTH + 3 * CONV_WIDTH
EPS = 1e-6

S_Q = ATTN_WIDTH
S_K = S_Q + KV_WIDTH
S_V = S_K + KV_WIDTH
S_B = S_V + CONV_WIDTH
S_C = S_B + CONV_WIDTH

LANES = 128
QB = WINDOW
KB = WINDOW + QB
CONV_CARRY_ROWS = 8
MASKED = -1e30
LOG2_E = 1.4426950408889634
SCORE_SCALE_LOG2 = HEAD_DIM ** -0.5 * LOG2_E
ROW_TILE = 512
FF_CHUNK = 1024
W_IN_CHUNK = 256
VMEM_LIMIT_BYTES = 56 * 1024 * 1024


def _rms(x):
    return x * lax.rsqrt(jnp.mean(x * x, axis=-1, keepdims=True) + EPS)


def _rope(t, tab, low_half):
    half = ROT_DIM // 2
    partner = jnp.where(low_half,
                        pltpu.roll(t, LANES - half, axis=1),
                        pltpu.roll(t, half, axis=1))
    return t * tab[:, 0:LANES] + partner * tab[:, LANES:2 * LANES]


def _low_half(tm):
    return lax.broadcasted_iota(jnp.int32, (tm, LANES), 1) % HEAD_DIM < ROT_DIM // 2


def _post_q(chunk, proj_buf, tab_ref, q_buf, *, tm):
    tab = tab_ref[...]
    for c in range(chunk * W_IN_CHUNK // LANES, (chunk + 1) * W_IN_CHUNK // LANES):
        qc = _rope(proj_buf[:, c * LANES:(c + 1) * LANES], tab, _low_half(tm)
                   ).astype(jnp.bfloat16)
        q_buf[2 * c] = qc[:, :HEAD_DIM]
        q_buf[2 * c + 1] = qc[:, HEAD_DIM:]


def _post_kv(proj_buf, tab_ref, k_buf, vt_buf, ok_ref, ov_ref, *, tm):
    kr = (_rope(proj_buf[:, S_Q:S_K], tab_ref[...], _low_half(tm)) * SCORE_SCALE_LOG2
          ).astype(jnp.bfloat16)
    for g in range(N_KV_HEADS):
        kg = kr[:, g * HEAD_DIM:(g + 1) * HEAD_DIM]
        k_buf[g, WINDOW:WINDOW + tm, :] = kg
        ok_ref[0, g] = kg[tm - WINDOW:, :]
    vt = proj_buf[:, S_K:S_V].T
    vt_buf[:, WINDOW:WINDOW + tm] = vt
    ov_ref[0] = vt[:, tm - WINDOW:]


def _post_conv(half, proj_buf, conv_w_ref, ou_ref, u_buf, ycf_buf, *, tm):
    cols = slice(half * W_IN_CHUNK, (half + 1) * W_IN_CHUNK)
    gate = lambda start: proj_buf[:, start + half * W_IN_CHUNK:start + (half + 1) * W_IN_CHUNK]
    u = gate(S_B) * gate(S_C)
    u_buf[CONV_CARRY_ROWS:CONV_CARRY_ROWS + tm, cols] = u
    ou_ref[0, :, cols] = u[tm - CONV_CARRY_ROWS:, :]
    cw = conv_w_ref[:, cols]
    y_conv = cw[2:3, :] * u
    y_conv += cw[1:2, :] * u_buf[CONV_CARRY_ROWS - 1:CONV_CARRY_ROWS - 1 + tm, cols]
    y_conv += cw[0:1, :] * u_buf[CONV_CARRY_ROWS - 2:CONV_CARRY_ROWS - 2 + tm, cols]
    ycf_buf[:, cols] = gate(S_V) * y_conv


def _post_conv_norm(ycf_buf, conv_g_ref, yc_buf):
    yc_buf[...] = (_rms(ycf_buf[...]) * conv_g_ref[...]).astype(jnp.bfloat16)


def _scores(item, first_tile, q_buf, k_buf, bias_buf):
    j, g = divmod(item, N_KV_HEADS)
    r0 = j * QB
    bias = bias_buf[jnp.where(first_tile, 0, 1)] if j == 0 else bias_buf[1]
    kg = k_buf[g, r0:r0 + KB, :]
    qg = q_buf[g * GROUP:(g + 1) * GROUP, r0:r0 + QB, :].reshape(GROUP * QB, HEAD_DIM)
    return lax.dot_general(kg, qg, (((1,), (1,)), ((), ())),
                           preferred_element_type=jnp.float32) + bias


def _softmax_pv(item, s, layer, sinks_ref, vt_buf, yat_buf):
    j, g = divmod(item, N_KV_HEADS)
    r0 = j * QB
    sink = jnp.concatenate(
        [jnp.full((1, QB), sinks_ref[layer, g * GROUP + i] * LOG2_E, jnp.float32)
         for i in range(GROUP)], axis=1)
    m = jnp.maximum(jnp.max(s, axis=0, keepdims=True), sink)
    e = jnp.exp2(s - m)
    den = jnp.sum(e, axis=0, keepdims=True) + jnp.exp2(sink - m)
    vt = vt_buf[g * HEAD_DIM:(g + 1) * HEAD_DIM, r0:r0 + KB].astype(jnp.bfloat16)
    o = jnp.dot(vt, e.astype(jnp.bfloat16), preferred_element_type=jnp.float32)
    o = o * (1.0 / den)
    for i in range(GROUP):
        hd = g * GROUP + i
        yat_buf[hd * HEAD_DIM:(hd + 1) * HEAD_DIM, r0:r0 + QB] = o[:, i * QB:(i + 1) * QB]


def _mixer_out_post(h_ref, attn_g_ref, w_out_ref, post_g_ref, o_ref, yc_buf, yat_buf):
    yat = yat_buf[...]
    inv = lax.rsqrt(jnp.mean(yat * yat, axis=0, keepdims=True) + EPS)
    ya_n = (yat * inv * attn_g_ref[...]).astype(jnp.bfloat16)
    mixed = lax.dot_general(ya_n, w_out_ref[0:ATTN_WIDTH, :], (((0,), (0,)), ((), ())),
                            preferred_element_type=jnp.float32)
    mixed += jnp.dot(yc_buf[...], w_out_ref[ATTN_WIDTH:, :], preferred_element_type=jnp.float32)
    o_ref[...] = h_ref[...] + _rms(mixed) * post_g_ref[...]


def _mixer_kernel(sinks_ref, h_in_ref, h_out_ref, tab_ref, ck_ref, cv_ref, cu_ref, cb_ref,
                  pre_g_ref, w_in_ref, conv_w_ref, attn_g_ref, conv_g_ref, w_out_ref, post_g_ref,
                  o_ref, ok_ref, ov_ref, ou_ref,
                  q0, k0, vt0, u0, yc0, q1, k1, vt1, u1, yc1, yat_buf, bias_buf, a_buf, proj_buf,
                  ycf_buf, *, layer, tm, n_tiles, n_steps):
    s = pl.program_id(0)
    tile_in = jnp.minimum(s, n_steps - 1) % n_tiles
    tile_out = jnp.maximum(s - 1, 0) % n_tiles
    slots = ((q0, k0, vt0, u0, yc0), (q1, k1, vt1, u1, yc1))

    @pl.when(s == 0)
    def _():
        key_idx = lax.broadcasted_iota(jnp.int32, (KB, QB), 0)
        qry_idx = lax.broadcasted_iota(jnp.int32, (KB, QB), 1)
        band = jnp.where((key_idx > qry_idx) & (key_idx <= qry_idx + WINDOW), 0.0, MASKED)
        band = jnp.concatenate([band] * GROUP, axis=1)
        bias_buf[1] = band
        bias_buf[0] = band + cb_ref[...]
        q1[...] = jnp.zeros(q1.shape, q1.dtype)
        k1[...] = jnp.zeros(k1.shape, k1.dtype)
        vt1[...] = jnp.zeros(vt1.shape, vt1.dtype)
        yc1[...] = jnp.zeros(yc1.shape, yc1.dtype)

    def step(cur, prev):
        q_c, k_c, vt_c, u_c, yc_c = cur
        q_p, k_p, vt_p, u_p, yc_p = prev

        @pl.when(tile_in == 0)
        def _():
            k_c[:, 0:WINDOW, :] = ck_ref[...]
            vt_c[:, 0:WINDOW] = cv_ref[...]
            u_c[0:CONV_CARRY_ROWS, :] = cu_ref[...]

        @pl.when(tile_in != 0)
        def _():
            k_c[:, 0:WINDOW, :] = k_p[:, tm:tm + WINDOW, :]
            vt_c[:, 0:WINDOW] = vt_p[:, tm:tm + WINDOW]
            u_c[0:CONV_CARRY_ROWS, :] = u_p[tm:tm + CONV_CARRY_ROWS, :]

        a_buf[...] = (_rms(h_in_ref[...]) * pre_g_ref[...]).astype(jnp.bfloat16)

        def w_in_chunk(c):
            cols = slice(c * W_IN_CHUNK, (c + 1) * W_IN_CHUNK)
            proj_buf[:, cols] = jnp.dot(a_buf[...], w_in_ref[:, cols],
                                        preferred_element_type=jnp.float32)

        q_of = functools.partial(_post_q, proj_buf=proj_buf, tab_ref=tab_ref, q_buf=q_c, tm=tm)
        conv_of = functools.partial(_post_conv, proj_buf=proj_buf, conv_w_ref=conv_w_ref,
                                    ou_ref=ou_ref, u_buf=u_c, ycf_buf=ycf_buf, tm=tm)

        def conv_tail():
            conv_of(1)
            _post_conv_norm(ycf_buf, conv_g_ref, yc_c)

        chunk_plan = [
            (0, lambda: q_of(0)), (1, lambda: q_of(1)),
            (5, None), (7, None), (3, lambda: conv_of(0)),
            (6, None), (8, None), (4, conv_tail),
            (2, lambda: _post_kv(proj_buf, tab_ref, k_c, vt_c, ok_ref, ov_ref, tm=tm)),
        ]
        assert sorted(c for c, _ in chunk_plan) == list(range(IN_WIDTH // W_IN_CHUNK))

        def run_chunk(entry):
            c, post = entry
            w_in_chunk(c)
            if post is not None:
                post()

        n_items = (tm // QB) * N_KV_HEADS
        first_out = tile_out == 0
        s_next = _scores(0, first_out, q_p, k_p, bias_buf)
        for item in range(n_items):
            s_cur = s_next
            if item + 1 < n_items:
                s_next = _scores(item + 1, first_out, q_p, k_p, bias_buf)
            if item < len(chunk_plan):
                run_chunk(chunk_plan[item])
            _softmax_pv(item, s_cur, layer, sinks_ref, vt_p, yat_buf)
        for entry in chunk_plan[n_items:]:
            run_chunk(entry)
        _mixer_out_post(h_out_ref, attn_g_ref, w_out_ref, post_g_ref, o_ref, yc_p, yat_buf)

    @pl.when(s % 2 == 0)
    def _():
        step(slots[0], slots[1])

    @pl.when(s % 2 == 1)
    def _():
        step(slots[1], slots[0])


def _mixer(h, tab, ck, cv, cu, cb, sinks, pre_g, w_in, conv_w, attn_g, conv_g, w_out, post_g,
           *, layer, batch, tm):
    rows = h.shape[0]
    n_steps = rows // tm
    n_tiles = n_steps // batch
    const = lambda s: (0, 0)
    const3 = lambda s: (0, 0, 0)
    of_layer = lambda s: (layer, 0, 0)
    in_map = lambda s: (jnp.minimum(s, n_steps - 1), 0)
    out_map = lambda s: (jnp.maximum(s - 1, 0), 0)
    seq_of_in = lambda s: jnp.minimum(s, n_steps - 1) // n_tiles
    resident = dict(pipeline_mode=pl.Buffered(1))
    kernel = functools.partial(_mixer_kernel, layer=layer, tm=tm, n_tiles=n_tiles,
                               n_steps=n_steps)
    slot = [
        pltpu.VMEM((N_Q_HEADS, tm, HEAD_DIM), jnp.bfloat16),
        pltpu.VMEM((N_KV_HEADS, WINDOW + tm, HEAD_DIM), jnp.bfloat16),
        pltpu.VMEM((KV_WIDTH, WINDOW + tm), jnp.float32),
        pltpu.VMEM((CONV_CARRY_ROWS + tm, CONV_WIDTH), jnp.float32),
        pltpu.VMEM((tm, CONV_WIDTH), jnp.bfloat16),
    ]
    return pl.pallas_call(
        kernel,
        grid=(n_steps + 1,),
        in_specs=[
            pl.BlockSpec(memory_space=pltpu.SMEM),
            pl.BlockSpec((tm, D_MODEL), in_map),
            pl.BlockSpec((tm, D_MODEL), out_map),
            pl.BlockSpec((tm, 2 * LANES),
                         lambda s: (jnp.minimum(s, n_steps - 1) % n_tiles, 0)),
            pl.BlockSpec((N_KV_HEADS, WINDOW, HEAD_DIM), const3),
            pl.BlockSpec((KV_WIDTH, WINDOW), const),
            pl.BlockSpec((CONV_CARRY_ROWS, CONV_WIDTH), const),
            pl.BlockSpec((KB, 1), const),
            pl.BlockSpec((None, 1, D_MODEL), of_layer),
            pl.BlockSpec((None, D_MODEL, IN_WIDTH), of_layer, **resident),
            pl.BlockSpec((None, CONV_K, CONV_WIDTH), of_layer),
            pl.BlockSpec((None, ATTN_WIDTH, 1), of_layer),
            pl.BlockSpec((None, 1, CONV_WIDTH), of_layer),
            pl.BlockSpec((None, D_MODEL, D_MODEL), of_layer, **resident),
            pl.BlockSpec((None, 1, D_MODEL), of_layer),
        ],
        out_specs=[
            pl.BlockSpec((tm, D_MODEL), out_map),
            pl.BlockSpec((1, N_KV_HEADS, WINDOW, HEAD_DIM), lambda s: (seq_of_in(s), 0, 0, 0)),
            pl.BlockSpec((1, KV_WIDTH, WINDOW), lambda s: (seq_of_in(s), 0, 0)),
            pl.BlockSpec((1, CONV_CARRY_ROWS, CONV_WIDTH), lambda s: (seq_of_in(s), 0, 0)),
        ],
        out_shape=[
            jax.ShapeDtypeStruct((rows, D_MODEL), jnp.float32),
            jax.ShapeDtypeStruct((batch, N_KV_HEADS, WINDOW, HEAD_DIM), jnp.bfloat16),
            jax.ShapeDtypeStruct((batch, KV_WIDTH, WINDOW), jnp.float32),
            jax.ShapeDtypeStruct((batch, CONV_CARRY_ROWS, CONV_WIDTH), jnp.float32),
        ],
        scratch_shapes=slot + slot + [
            pltpu.VMEM((ATTN_WIDTH, tm), jnp.float32),
            pltpu.VMEM((2, KB, GROUP * QB), jnp.float32),
            pltpu.VMEM((tm, D_MODEL), jnp.bfloat16),
            pltpu.VMEM((tm, IN_WIDTH), jnp.float32),
            pltpu.VMEM((tm, CONV_WIDTH), jnp.float32),
        ],
        compiler_params=pltpu.CompilerParams(
            dimension_semantics=("arbitrary",),
            vmem_limit_bytes=VMEM_LIMIT_BYTES),
        name="mixer",
    )(sinks, h, h, tab, ck, cv, cu, cb, pre_g, w_in, conv_w, attn_g, conv_g, w_out, post_g)


def _mlp_kernel(h_ref, pre_g_ref, w_up_ref, w_down_ref, post_g_ref, o_ref):
    h = h_ref[...]
    a = (_rms(h) * pre_g_ref[...]).astype(jnp.bfloat16)
    acc = jnp.zeros(h.shape, jnp.float32)
    for c in range(D_FF // FF_CHUNK):
        cols = slice(c * FF_CHUNK, (c + 1) * FF_CHUNK)
        up = jnp.dot(a, w_up_ref[:, cols], preferred_element_type=jnp.float32)
        act = jnp.square(jnp.maximum(up, 0.0)).astype(jnp.bfloat16)
        acc = acc + jnp.dot(act, w_down_ref[cols, :], preferred_element_type=jnp.float32)
    o_ref[...] = h + _rms(acc) * post_g_ref[...]


def _mlp(h, pre_g, w_up, w_down, post_g, *, layer, tm):
    rows = h.shape[0]
    of_layer = lambda i: (layer, 0, 0)
    resident = dict(pipeline_mode=pl.Buffered(1))
    return pl.pallas_call(
        _mlp_kernel,
        grid=(rows // tm,),
        in_specs=[
            pl.BlockSpec((tm, D_MODEL), lambda i: (i, 0)),
            pl.BlockSpec((None, 1, D_MODEL), of_layer),
            pl.BlockSpec((None, D_MODEL, D_FF), of_layer, **resident),
            pl.BlockSpec((None, D_FF, D_MODEL), of_layer, **resident),
            pl.BlockSpec((None, 1, D_MODEL), of_layer),
        ],
        out_specs=pl.BlockSpec((tm, D_MODEL), lambda i: (i, 0)),
        out_shape=jax.ShapeDtypeStruct((rows, D_MODEL), jnp.float32),
        compiler_params=pltpu.CompilerParams(
            dimension_semantics=("arbitrary",),
            vmem_limit_bytes=VMEM_LIMIT_BYTES),
        name="mlp",
    )(h, pre_g, w_up, w_down, post_g)


def _rope_table(first_pos, n_pos):
    pos = (first_pos + jnp.arange(n_pos)).astype(jnp.float32)
    inv_freq = jnp.power(jnp.float32(ROPE_THETA),
                         -jnp.arange(0, ROT_DIM, 2, dtype=jnp.float32) / ROT_DIM)
    half = ROT_DIM // 2
    d = jnp.arange(LANES) % HEAD_DIM
    ang = pos[:, None] * inv_freq[d % half][None, :]
    c = jnp.where(d < ROT_DIM, jnp.cos(ang), 1.0)
    s = jnp.where(d < half, -jnp.sin(ang), jnp.where(d < ROT_DIM, jnp.sin(ang), 0.0))
    return jnp.concatenate([c, s], axis=1)


def kernel(x, meta_tokens, mix_pre_g, w_in, conv_w, sinks, attn_out_g, conv_out_g, w_out,
           mix_post_g, mlp_pre_g, w_up, w_down, mlp_post_g):
    batch, seq, _ = x.shape
    depth = w_in.shape[0]
    bf16 = jnp.bfloat16
    f32 = jnp.float32
    lead = QB - N_META

    tab_meta = jnp.pad(_rope_table(0, N_META), ((lead, 0), (0, 0)))
    tab_x = _rope_table(N_META, seq)

    hx = x.reshape(batch * seq, D_MODEL)
    hm = jnp.pad(meta_tokens.astype(x.dtype), ((lead, 0), (0, 0)))

    masked = jnp.full((WINDOW, 1), MASKED, f32)
    tail_valid = jnp.where(jnp.arange(WINDOW)[:, None] >= lead, 0.0, MASKED).astype(f32)
    cb_meta = jnp.concatenate([masked, tail_valid], axis=0)
    cb_x = jnp.concatenate([tail_valid, jnp.zeros((QB, 1), f32)], axis=0)
    zero_k = jnp.zeros((N_KV_HEADS, WINDOW, HEAD_DIM), bf16)
    zero_vt = jnp.zeros((KV_WIDTH, WINDOW), f32)
    zero_u = jnp.zeros((CONV_CARRY_ROWS, CONV_WIDTH), f32)

    rows_of = lambda v: v.reshape(depth, 1, -1)
    mixer_w = (sinks, rows_of(mix_pre_g), w_in.astype(bf16), conv_w,
               attn_out_g.reshape(depth, -1, 1), rows_of(conv_out_g), w_out.astype(bf16),
               rows_of(mix_post_g))
    mlp_w = (rows_of(mlp_pre_g), w_up.astype(bf16), w_down.astype(bf16), rows_of(mlp_post_g))
    for l in range(depth):
        hm, mk, mv, mu = _mixer(hm, tab_meta, zero_k, zero_vt, zero_u, cb_meta, *mixer_w,
                                layer=l, batch=1, tm=QB)
        hx = _mixer(hx, tab_x, mk[0], mv[0], mu[0], cb_x, *mixer_w,
                    layer=l, batch=batch, tm=ROW_TILE)[0]
        if l + 1 < depth:
            hm = _mlp(hm, *mlp_w, layer=l, tm=QB)
        hx = _mlp(hx, *mlp_w, layer=l, tm=ROW_TILE)
    return hx.reshape(batch, seq, D_MODEL)
```

```python
import functools

import jax
import jax.numpy as jnp
from jax import lax
from jax.experimental import pallas as pl
from jax.experimental.pallas import tpu as pltpu

D_MODEL = 1024
N_META = 16
ATTN_WIDTH = 512
CONV_WIDTH = 512
HEAD_DIM = 64
N_Q_HEADS = 8
N_KV_HEADS = 2
GROUP = N_Q_HEADS // N_KV_HEADS
KV_WIDTH = N_KV_HEADS * HEAD_DIM
CONV_K = 3
WINDOW = 128
ROPE_THETA = 500000.0
ROT_DIM = HEAD_DIM // 4
D_FF = 4 * D_MODEL
IN_WIDTH = ATTN_WIDTH + 2 * KV_WIDTH + 3 * CONV_WIDTH
EPS = 1e-6

S_Q = ATTN_WIDTH
S_K = S_Q + KV_WIDTH
S_V = S_K + KV_WIDTH
S_B = S_V + CONV_WIDTH
S_C = S_B + CONV_WIDTH

LANES = 128
QB = WINDOW
KB = WINDOW + QB
CONV_CARRY_ROWS = 8
MASKED = -1e30
LOG2_E = 1.4426950408889634
SCORE_SCALE_LOG2 = HEAD_DIM ** -0.5 * LOG2_E
ROW_TILE = 512
FF_CHUNK = 1024
MLP_TILE = 1024
MLP_SUB_ROWS = 512
W_IN_CHUNK = 256
VMEM_LIMIT_BYTES = 56 * 1024 * 1024


def _rms(x):
    return x * lax.rsqrt(jnp.mean(x * x, axis=-1, keepdims=True) + EPS)


def _rope(t, tab, low_half):
    half = ROT_DIM // 2
    partner = jnp.where(low_half,
                        pltpu.roll(t, LANES - half, axis=1),
                        pltpu.roll(t, half, axis=1))
    return t * tab[:, 0:LANES] + partner * tab[:, LANES:2 * LANES]


def _low_half(tm):
    return lax.broadcasted_iota(jnp.int32, (tm, LANES), 1) % HEAD_DIM < ROT_DIM // 2


def _post_q(chunk, proj_buf, tab_ref, q_buf, *, tm):
    tab = tab_ref[...]
    for c in range(chunk * W_IN_CHUNK // LANES, (chunk + 1) * W_IN_CHUNK // LANES):
        qc = _rope(proj_buf[:, c * LANES:(c + 1) * LANES], tab, _low_half(tm)
                   ).astype(jnp.bfloat16)
        q_buf[2 * c] = qc[:, :HEAD_DIM]
        q_buf[2 * c + 1] = qc[:, HEAD_DIM:]


def _post_kv(proj_buf, tab_ref, k_buf, vt_buf, ok_ref, ov_ref, *, tm):
    kr = (_rope(proj_buf[:, S_Q:S_K], tab_ref[...], _low_half(tm)) * SCORE_SCALE_LOG2
          ).astype(jnp.bfloat16)
    for g in range(N_KV_HEADS):
        kg = kr[:, g * HEAD_DIM:(g + 1) * HEAD_DIM]
        k_buf[g, WINDOW:WINDOW + tm, :] = kg
        ok_ref[0, g] = kg[tm - WINDOW:, :]
    vt = proj_buf[:, S_K:S_V].T
    vt_buf[:, WINDOW:WINDOW + tm] = vt
    ov_ref[0] = vt[:, tm - WINDOW:]


def _post_conv(half, proj_buf, conv_w_ref, ou_ref, u_buf, ycf_buf, *, tm):
    cols = slice(half * W_IN_CHUNK, (half + 1) * W_IN_CHUNK)
    gate = lambda start: proj_buf[:, start + half * W_IN_CHUNK:start + (half + 1) * W_IN_CHUNK]
    u = gate(S_B) * gate(S_C)
    u_buf[CONV_CARRY_ROWS:CONV_CARRY_ROWS + tm, cols] = u
    ou_ref[0, :, cols] = u[tm - CONV_CARRY_ROWS:, :]
    cw = conv_w_ref[:, cols]
    y_conv = cw[2:3, :] * u
    y_conv += cw[1:2, :] * u_buf[CONV_CARRY_ROWS - 1:CONV_CARRY_ROWS - 1 + tm, cols]
    y_conv += cw[0:1, :] * u_buf[CONV_CARRY_ROWS - 2:CONV_CARRY_ROWS - 2 + tm, cols]
    ycf_buf[:, cols] = gate(S_V) * y_conv


def _post_conv_norm(ycf_buf, conv_g_ref, yc_buf):
    yc_buf[...] = (_rms(ycf_buf[...]) * conv_g_ref[...]).astype(jnp.bfloat16)


def _scores(item, first_tile, q_buf, k_buf, bias_buf):
    j, g = divmod(item, N_KV_HEADS)
    r0 = j * QB
    bias = bias_buf[jnp.where(first_tile, 0, 1)] if j == 0 else bias_buf[1]
    kg = k_buf[g, r0:r0 + KB, :]
    qg = q_buf[g * GROUP:(g + 1) * GROUP, r0:r0 + QB, :].reshape(GROUP * QB, HEAD_DIM)
    return lax.dot_general(kg, qg, (((1,), (1,)), ((), ())),
                           preferred_element_type=jnp.float32) + bias


def _softmax_pv(item, s, layer, sinks_ref, vt_buf, yat_buf):
    j, g = divmod(item, N_KV_HEADS)
    r0 = j * QB
    sink = jnp.concatenate(
        [jnp.full((1, QB), sinks_ref[layer, g * GROUP + i] * LOG2_E, jnp.float32)
         for i in range(GROUP)], axis=1)
    m = jnp.maximum(jnp.max(s, axis=0, keepdims=True), sink)
    e = jnp.exp2(s - m)
    den = jnp.sum(e, axis=0, keepdims=True) + jnp.exp2(sink - m)
    vt = vt_buf[g * HEAD_DIM:(g + 1) * HEAD_DIM, r0:r0 + KB].astype(jnp.bfloat16)
    o = jnp.dot(vt, e.astype(jnp.bfloat16), preferred_element_type=jnp.float32)
    o = o * (1.0 / den)
    for i in range(GROUP):
        hd = g * GROUP + i
        yat_buf[hd * HEAD_DIM:(hd + 1) * HEAD_DIM, r0:r0 + QB] = o[:, i * QB:(i + 1) * QB]


def _mixer_out_post(h_ref, attn_g_ref, w_out_ref, post_g_ref, o_ref, yc_buf, yat_buf):
    yat = yat_buf[...]
    inv = lax.rsqrt(jnp.mean(yat * yat, axis=0, keepdims=True) + EPS)
    ya_n = (yat * inv * attn_g_ref[...]).astype(jnp.bfloat16)
    mixed = lax.dot_general(ya_n, w_out_ref[0:ATTN_WIDTH, :], (((0,), (0,)), ((), ())),
                            preferred_element_type=jnp.float32)
    mixed += jnp.dot(yc_buf[...], w_out_ref[ATTN_WIDTH:, :], preferred_element_type=jnp.float32)
    o_ref[...] = h_ref[...] + _rms(mixed) * post_g_ref[...]


def _mixer_kernel(sinks_ref, h_in_ref, h_out_ref, tab_ref, ck_ref, cv_ref, cu_ref, cb_ref,
                  pre_g_ref, w_in_ref, conv_w_ref, attn_g_ref, conv_g_ref, w_out_ref, post_g_ref,
                  o_ref, ok_ref, ov_ref, ou_ref,
                  q0, k0, vt0, u0, yc0, q1, k1, vt1, u1, yc1, yat_buf, bias_buf, a_buf, proj_buf,
                  ycf_buf, *, layer, tm, n_tiles, n_steps):
    s = pl.program_id(0)
    tile_in = jnp.minimum(s, n_steps - 1) % n_tiles
    tile_out = jnp.maximum(s - 1, 0) % n_tiles
    slots = ((q0, k0, vt0, u0, yc0), (q1, k1, vt1, u1, yc1))

    @pl.when(s == 0)
    def _():
        key_idx = lax.broadcasted_iota(jnp.int32, (KB, QB), 0)
        qry_idx = lax.broadcasted_iota(jnp.int32, (KB, QB), 1)
        band = jnp.where((key_idx > qry_idx) & (key_idx <= qry_idx + WINDOW), 0.0, MASKED)
        band = jnp.concatenate([band] * GROUP, axis=1)
        bias_buf[1] = band
        bias_buf[0] = band + cb_ref[...]
        q1[...] = jnp.zeros(q1.shape, q1.dtype)
        k1[...] = jnp.zeros(k1.shape, k1.dtype)
        vt1[...] = jnp.zeros(vt1.shape, vt1.dtype)
        yc1[...] = jnp.zeros(yc1.shape, yc1.dtype)

    def step(cur, prev):
        q_c, k_c, vt_c, u_c, yc_c = cur
        q_p, k_p, vt_p, u_p, yc_p = prev

        @pl.when(tile_in == 0)
        def _():
            k_c[:, 0:WINDOW, :] = ck_ref[...]
            vt_c[:, 0:WINDOW] = cv_ref[...]
            u_c[0:CONV_CARRY_ROWS, :] = cu_ref[...]

        @pl.when(tile_in != 0)
        def _():
            k_c[:, 0:WINDOW, :] = k_p[:, tm:tm + WINDOW, :]
            vt_c[:, 0:WINDOW] = vt_p[:, tm:tm + WINDOW]
            u_c[0:CONV_CARRY_ROWS, :] = u_p[tm:tm + CONV_CARRY_ROWS, :]

        a_buf[...] = (_rms(h_in_ref[...]) * pre_g_ref[...]).astype(jnp.bfloat16)

        def w_in_chunk(c):
            cols = slice(c * W_IN_CHUNK, (c + 1) * W_IN_CHUNK)
            proj_buf[:, cols] = jnp.dot(a_buf[...], w_in_ref[:, cols],
                                        preferred_element_type=jnp.float32)

        q_of = functools.partial(_post_q, proj_buf=proj_buf, tab_ref=tab_ref, q_buf=q_c, tm=tm)
        conv_of = functools.partial(_post_conv, proj_buf=proj_buf, conv_w_ref=conv_w_ref,
                                    ou_ref=ou_ref, u_buf=u_c, ycf_buf=ycf_buf, tm=tm)

        def conv_tail():
            conv_of(1)
            _post_conv_norm(ycf_buf, conv_g_ref, yc_c)

        chunk_plan = [
            (0, lambda: q_of(0)), (1, lambda: q_of(1)),
            (5, None), (7, None), (3, lambda: conv_of(0)),
            (6, None), (8, None), (4, conv_tail),
            (2, lambda: _post_kv(proj_buf, tab_ref, k_c, vt_c, ok_ref, ov_ref, tm=tm)),
        ]
        assert sorted(c for c, _ in chunk_plan) == list(range(IN_WIDTH // W_IN_CHUNK))

        def run_chunk(entry):
            c, post = entry
            w_in_chunk(c)
            if post is not None:
                post()

        n_items = (tm // QB) * N_KV_HEADS
        first_out = tile_out == 0
        s_next = _scores(0, first_out, q_p, k_p, bias_buf)
        for item in range(n_items):
            s_cur = s_next
            if item + 1 < n_items:
                s_next = _scores(item + 1, first_out, q_p, k_p, bias_buf)
            if item < len(chunk_plan):
                run_chunk(chunk_plan[item])
            _softmax_pv(item, s_cur, layer, sinks_ref, vt_p, yat_buf)
        for entry in chunk_plan[n_items:]:
            run_chunk(entry)
        _mixer_out_post(h_out_ref, attn_g_ref, w_out_ref, post_g_ref, o_ref, yc_p, yat_buf)

    @pl.when(s % 2 == 0)
    def _():
        step(slots[0], slots[1])

    @pl.when(s % 2 == 1)
    def _():
        step(slots[1], slots[0])


def _mixer(h, tab, ck, cv, cu, cb, sinks, pre_g, w_in, conv_w, attn_g, conv_g, w_out, post_g,
           *, layer, batch, tm):
    rows = h.shape[0]
    n_steps = rows // tm
    n_tiles = n_steps // batch
    const = lambda s: (0, 0)
    const3 = lambda s: (0, 0, 0)
    of_layer = lambda s: (layer, 0, 0)
    in_map = lambda s: (jnp.minimum(s, n_steps - 1), 0)
    out_map = lambda s: (jnp.maximum(s - 1, 0), 0)
    seq_of_in = lambda s: jnp.minimum(s, n_steps - 1) // n_tiles
    resident = dict(pipeline_mode=pl.Buffered(1))
    kernel = functools.partial(_mixer_kernel, layer=layer, tm=tm, n_tiles=n_tiles,
                               n_steps=n_steps)
    slot = [
        pltpu.VMEM((N_Q_HEADS, tm, HEAD_DIM), jnp.bfloat16),
        pltpu.VMEM((N_KV_HEADS, WINDOW + tm, HEAD_DIM), jnp.bfloat16),
        pltpu.VMEM((KV_WIDTH, WINDOW + tm), jnp.float32),
        pltpu.VMEM((CONV_CARRY_ROWS + tm, CONV_WIDTH), jnp.float32),
        pltpu.VMEM((tm, CONV_WIDTH), jnp.bfloat16),
    ]
    return pl.pallas_call(
        kernel,
        grid=(n_steps + 1,),
        in_specs=[
            pl.BlockSpec(memory_space=pltpu.SMEM),
            pl.BlockSpec((tm, D_MODEL), in_map),
            pl.BlockSpec((tm, D_MODEL), out_map),
            pl.BlockSpec((tm, 2 * LANES),
                         lambda s: (jnp.minimum(s, n_steps - 1) % n_tiles, 0)),
            pl.BlockSpec((N_KV_HEADS, WINDOW, HEAD_DIM), const3),
            pl.BlockSpec((KV_WIDTH, WINDOW), const),
            pl.BlockSpec((CONV_CARRY_ROWS, CONV_WIDTH), const),
            pl.BlockSpec((KB, 1), const),
            pl.BlockSpec((None, 1, D_MODEL), of_layer),
            pl.BlockSpec((None, D_MODEL, IN_WIDTH), of_layer, **resident),
            pl.BlockSpec((None, CONV_K, CONV_WIDTH), of_layer),
            pl.BlockSpec((None, ATTN_WIDTH, 1), of_layer),
            pl.BlockSpec((None, 1, CONV_WIDTH), of_layer),
            pl.BlockSpec((None, D_MODEL, D_MODEL), of_layer, **resident),
            pl.BlockSpec((None, 1, D_MODEL), of_layer),
        ],
        out_specs=[
            pl.BlockSpec((tm, D_MODEL), out_map),
            pl.BlockSpec((1, N_KV_HEADS, WINDOW, HEAD_DIM), lambda s: (seq_of_in(s), 0, 0, 0)),
            pl.BlockSpec((1, KV_WIDTH, WINDOW), lambda s: (seq_of_in(s), 0, 0)),
            pl.BlockSpec((1, CONV_CARRY_ROWS, CONV_WIDTH), lambda s: (seq_of_in(s), 0, 0)),
        ],
        out_shape=[
            jax.ShapeDtypeStruct((rows, D_MODEL), jnp.float32),
            jax.ShapeDtypeStruct((batch, N_KV_HEADS, WINDOW, HEAD_DIM), jnp.bfloat16),
            jax.ShapeDtypeStruct((batch, KV_WIDTH, WINDOW), jnp.float32),
            jax.ShapeDtypeStruct((batch, CONV_CARRY_ROWS, CONV_WIDTH), jnp.float32),
        ],
        scratch_shapes=slot + slot + [
            pltpu.VMEM((ATTN_WIDTH, tm), jnp.float32),
            pltpu.VMEM((2, KB, GROUP * QB), jnp.float32),
            pltpu.VMEM((tm, D_MODEL), jnp.bfloat16),
            pltpu.VMEM((tm, IN_WIDTH), jnp.float32),
            pltpu.VMEM((tm, CONV_WIDTH), jnp.float32),
        ],
        compiler_params=pltpu.CompilerParams(
            dimension_semantics=("arbitrary",),
            vmem_limit_bytes=VMEM_LIMIT_BYTES),
        name="mixer",
    )(sinks, h, h, tab, ck, cv, cu, cb, pre_g, w_in, conv_w, attn_g, conv_g, w_out, post_g)


def _mlp_kernel(h_ref, pre_g_ref, w_up_ref, w_down_ref, post_g_ref, o_ref, *, tm):
    sub = min(MLP_SUB_ROWS, tm)
    subs = [slice(r, r + sub) for r in range(0, tm, sub)]
    a = [(_rms(h_ref[rows, :]) * pre_g_ref[...]).astype(jnp.bfloat16) for rows in subs]
    for a_sub, rows in zip(a, subs):
        acc = jnp.zeros((sub, D_MODEL), jnp.float32)
        for c in range(D_FF // FF_CHUNK):
            cols = slice(c * FF_CHUNK, (c + 1) * FF_CHUNK)
            up = jnp.dot(a_sub, w_up_ref[:, cols], preferred_element_type=jnp.float32)
            act = jnp.square(jnp.maximum(up, 0.0)).astype(jnp.bfloat16)
            acc = acc + jnp.dot(act, w_down_ref[cols, :], preferred_element_type=jnp.float32)
        o_ref[rows, :] = h_ref[rows, :] + _rms(acc) * post_g_ref[...]


def _mlp(h, pre_g, w_up, w_down, post_g, *, layer, tm):
    rows = h.shape[0]
    of_layer = lambda i: (layer, 0, 0)
    resident = dict(pipeline_mode=pl.Buffered(1))
    return pl.pallas_call(
        functools.partial(_mlp_kernel, tm=tm),
        grid=(rows // tm,),
        in_specs=[
            pl.BlockSpec((tm, D_MODEL), lambda i: (i, 0)),
            pl.BlockSpec((None, 1, D_MODEL), of_layer),
            pl.BlockSpec((None, D_MODEL, D_FF), of_layer, **resident),
            pl.BlockSpec((None, D_FF, D_MODEL), of_layer, **resident),
            pl.BlockSpec((None, 1, D_MODEL), of_layer),
        ],
        out_specs=pl.BlockSpec((tm, D_MODEL), lambda i: (i, 0)),
        out_shape=jax.ShapeDtypeStruct((rows, D_MODEL), jnp.float32),
        compiler_params=pltpu.CompilerParams(
            dimension_semantics=("arbitrary",),
            vmem_limit_bytes=VMEM_LIMIT_BYTES),
        name="mlp",
    )(h, pre_g, w_up, w_down, post_g)


def _rope_table(first_pos, n_pos):
    pos = (first_pos + jnp.arange(n_pos)).astype(jnp.float32)
    inv_freq = jnp.power(jnp.float32(ROPE_THETA),
                         -jnp.arange(0, ROT_DIM, 2, dtype=jnp.float32) / ROT_DIM)
    half = ROT_DIM // 2
    d = jnp.arange(LANES) % HEAD_DIM
    ang = pos[:, None] * inv_freq[d % half][None, :]
    c = jnp.where(d < ROT_DIM, jnp.cos(ang), 1.0)
    s = jnp.where(d < half, -jnp.sin(ang), jnp.where(d < ROT_DIM, jnp.sin(ang), 0.0))
    return jnp.concatenate([c, s], axis=1)


def kernel(x, meta_tokens, mix_pre_g, w_in, conv_w, sinks, attn_out_g, conv_out_g, w_out,
           mix_post_g, mlp_pre_g, w_up, w_down, mlp_post_g):
    batch, seq, _ = x.shape
    depth = w_in.shape[0]
    bf16 = jnp.bfloat16
    f32 = jnp.float32
    lead = QB - N_META

    tab_meta = jnp.pad(_rope_table(0, N_META), ((lead, 0), (0, 0)))
    tab_x = _rope_table(N_META, seq)

    hx = x.reshape(batch * seq, D_MODEL)
    hm = jnp.pad(meta_tokens.astype(x.dtype), ((lead, 0), (0, 0)))

    masked = jnp.full((WINDOW, 1), MASKED, f32)
    tail_valid = jnp.where(jnp.arange(WINDOW)[:, None] >= lead, 0.0, MASKED).astype(f32)
    cb_meta = jnp.concatenate([masked, tail_valid], axis=0)
    cb_x = jnp.concatenate([tail_valid, jnp.zeros((QB, 1), f32)], axis=0)
    zero_k = jnp.zeros((N_KV_HEADS, WINDOW, HEAD_DIM), bf16)
    zero_vt = jnp.zeros((KV_WIDTH, WINDOW), f32)
    zero_u = jnp.zeros((CONV_CARRY_ROWS, CONV_WIDTH), f32)

    rows_of = lambda v: v.reshape(depth, 1, -1)
    mixer_w = (sinks, rows_of(mix_pre_g), w_in.astype(bf16), conv_w,
               attn_out_g.reshape(depth, -1, 1), rows_of(conv_out_g), w_out.astype(bf16),
               rows_of(mix_post_g))
    mlp_w = (rows_of(mlp_pre_g), w_up.astype(bf16), w_down.astype(bf16), rows_of(mlp_post_g))
    for l in range(depth):
        hm, mk, mv, mu = _mixer(hm, tab_meta, zero_k, zero_vt, zero_u, cb_meta, *mixer_w,
                                layer=l, batch=1, tm=QB)
        hx = _mixer(hx, tab_x, mk[0], mv[0], mu[0], cb_x, *mixer_w,
                    layer=l, batch=batch, tm=ROW_TILE)[0]
        if l + 1 < depth:
            hm = _mlp(hm, *mlp_w, layer=l, tm=QB)
        hx = _mlp(hx, *mlp_w, layer=l, tm=MLP_TILE)
    return hx.reshape(batch, seq, D_MODEL)
```

```python
import functools

import jax
import jax.numpy as jnp
import numpy as np
from jax import lax
from jax.experimental import pallas as pl
from jax.experimental.pallas import tpu as pltpu

D_MODEL = 1024
N_META = 16
ATTN_WIDTH = 512
CONV_WIDTH = 512
HEAD_DIM = 64
N_Q_HEADS = 8
N_KV_HEADS = 2
GROUP = N_Q_HEADS // N_KV_HEADS
KV_WIDTH = N_KV_HEADS * HEAD_DIM
CONV_K = 3
WINDOW = 128
ROPE_THETA = 500000.0
ROT_DIM = HEAD_DIM // 4
D_FF = 4 * D_MODEL
IN_WIDTH = ATTN_WIDTH + 2 * KV_WIDTH + 3 * CONV_WIDTH
EPS = 1e-6

S_Q = ATTN_WIDTH
S_K = S_Q + KV_WIDTH
S_V = S_K + KV_WIDTH
S_B = S_V + CONV_WIDTH
S_C = S_B + CONV_WIDTH

LANES = 128
QB = WINDOW
KB = WINDOW + QB
CONV_CARRY_ROWS = 8
MASKED = -1e30
LOG2_E = 1.4426950408889634
SCORE_SCALE_LOG2 = HEAD_DIM ** -0.5 * LOG2_E
ROW_TILE = 512
FF_CHUNK = 1024
MLP_TILE = 1024
MLP_SUB_ROWS = 512
W_IN_CHUNK = 256
VMEM_LIMIT_BYTES = 56 * 1024 * 1024


def _rms(x):
    return x * lax.rsqrt(jnp.mean(x * x, axis=-1, keepdims=True) + EPS)


def _rope(t, tab, low_half):
    half = ROT_DIM // 2
    partner = jnp.where(low_half,
                        pltpu.roll(t, LANES - half, axis=1),
                        pltpu.roll(t, half, axis=1))
    return t * tab[:, 0:LANES] + partner * tab[:, LANES:2 * LANES]


def _low_half(tm):
    return lax.broadcasted_iota(jnp.int32, (tm, LANES), 1) % HEAD_DIM < ROT_DIM // 2


def _post_q(chunk, proj_buf, tab_ref, q_buf, *, tm):
    tab = tab_ref[...]
    for c in range(chunk * W_IN_CHUNK // LANES, (chunk + 1) * W_IN_CHUNK // LANES):
        qc = _rope(proj_buf[:, c * LANES:(c + 1) * LANES], tab, _low_half(tm)
                   ).astype(jnp.bfloat16)
        q_buf[2 * c] = qc[:, :HEAD_DIM]
        q_buf[2 * c + 1] = qc[:, HEAD_DIM:]


def _post_kv(proj_buf, tab_ref, k_buf, vt_buf, ok_ref, ov_ref, *, tm):
    kr = (_rope(proj_buf[:, S_Q:S_K], tab_ref[...], _low_half(tm)) * SCORE_SCALE_LOG2
          ).astype(jnp.bfloat16)
    for g in range(N_KV_HEADS):
        kg = kr[:, g * HEAD_DIM:(g + 1) * HEAD_DIM]
        k_buf[g, WINDOW:WINDOW + tm, :] = kg
        ok_ref[0, g] = kg[tm - WINDOW:, :]
    vt = proj_buf[:, S_K:S_V].T
    vt_buf[:, WINDOW:WINDOW + tm] = vt
    ov_ref[0] = vt[:, tm - WINDOW:]


def _post_conv(half, proj_buf, conv_w_ref, ou_ref, u_buf, ycf_buf, *, tm):
    cols = slice(half * W_IN_CHUNK, (half + 1) * W_IN_CHUNK)
    gate = lambda start: proj_buf[:, start + half * W_IN_CHUNK:start + (half + 1) * W_IN_CHUNK]
    u = gate(S_B) * gate(S_C)
    u_buf[CONV_CARRY_ROWS:CONV_CARRY_ROWS + tm, cols] = u
    ou_ref[0, :, cols] = u[tm - CONV_CARRY_ROWS:, :]
    cw = conv_w_ref[:, cols]
    y_conv = cw[2:3, :] * u
    y_conv += cw[1:2, :] * u_buf[CONV_CARRY_ROWS - 1:CONV_CARRY_ROWS - 1 + tm, cols]
    y_conv += cw[0:1, :] * u_buf[CONV_CARRY_ROWS - 2:CONV_CARRY_ROWS - 2 + tm, cols]
    ycf_buf[:, cols] = gate(S_V) * y_conv


def _post_conv_norm(ycf_buf, conv_g_ref, yc_buf):
    yc_buf[...] = (_rms(ycf_buf[...]) * conv_g_ref[...]).astype(jnp.bfloat16)


def _scores(item, first_tile, q_buf, k_buf, bias_buf):
    j, g = divmod(item, N_KV_HEADS)
    r0 = j * QB
    bias = bias_buf[jnp.where(first_tile, 0, 1)] if j == 0 else bias_buf[1]
    kg = k_buf[g, r0:r0 + KB, :]
    qg = q_buf[g * GROUP:(g + 1) * GROUP, r0:r0 + QB, :].reshape(GROUP * QB, HEAD_DIM)
    return lax.dot_general(kg, qg, (((1,), (1,)), ((), ())),
                           preferred_element_type=jnp.float32) + bias


def _softmax_pv(item, s, layer, sinks_ref, vt_buf, yat_buf):
    j, g = divmod(item, N_KV_HEADS)
    r0 = j * QB
    sink = jnp.concatenate(
        [jnp.full((1, QB), sinks_ref[layer, g * GROUP + i] * LOG2_E, jnp.float32)
         for i in range(GROUP)], axis=1)
    m = jnp.maximum(jnp.max(s, axis=0, keepdims=True), sink)
    e = jnp.exp2(s - m)
    den = jnp.sum(e, axis=0, keepdims=True) + jnp.exp2(sink - m)
    vt = vt_buf[g * HEAD_DIM:(g + 1) * HEAD_DIM, r0:r0 + KB].astype(jnp.bfloat16)
    o = jnp.dot(vt, e.astype(jnp.bfloat16), preferred_element_type=jnp.float32)
    o = o * (1.0 / den)
    for i in range(GROUP):
        hd = g * GROUP + i
        yat_buf[hd * HEAD_DIM:(hd + 1) * HEAD_DIM, r0:r0 + QB] = o[:, i * QB:(i + 1) * QB]


def _mixer_out_post(h_ref, attn_g_ref, w_out_ref, post_g_ref, o_ref, yc_buf, yat_buf):
    yat = yat_buf[...]
    inv = lax.rsqrt(jnp.mean(yat * yat, axis=0, keepdims=True) + EPS)
    ya_n = (yat * inv * attn_g_ref[...]).astype(jnp.bfloat16)
    mixed = lax.dot_general(ya_n, w_out_ref[0:ATTN_WIDTH, :], (((0,), (0,)), ((), ())),
                            preferred_element_type=jnp.float32)
    mixed += jnp.dot(yc_buf[...], w_out_ref[ATTN_WIDTH:, :], preferred_element_type=jnp.float32)
    o_ref[...] = h_ref[...] + _rms(mixed) * post_g_ref[...]


def _mixer_kernel(sinks_ref, h_in_ref, h_out_ref, tab_ref, ck_ref, cv_ref, cu_ref, cb_ref,
                  pre_g_ref, w_in_ref, conv_w_ref, attn_g_ref, conv_g_ref, w_out_ref, post_g_ref,
                  o_ref, ok_ref, ov_ref, ou_ref,
                  q0, k0, vt0, u0, yc0, q1, k1, vt1, u1, yc1, yat_buf, bias_buf, a_buf, proj_buf,
                  ycf_buf, *, layer, tm, n_tiles, n_steps):
    s = pl.program_id(0)
    tile_in = jnp.minimum(s, n_steps - 1) % n_tiles
    tile_out = jnp.maximum(s - 1, 0) % n_tiles
    slots = ((q0, k0, vt0, u0, yc0), (q1, k1, vt1, u1, yc1))

    @pl.when(s == 0)
    def _():
        key_idx = lax.broadcasted_iota(jnp.int32, (KB, QB), 0)
        qry_idx = lax.broadcasted_iota(jnp.int32, (KB, QB), 1)
        band = jnp.where((key_idx > qry_idx) & (key_idx <= qry_idx + WINDOW), 0.0, MASKED)
        band = jnp.concatenate([band] * GROUP, axis=1)
        bias_buf[1] = band
        bias_buf[0] = band + cb_ref[...]
        q1[...] = jnp.zeros(q1.shape, q1.dtype)
        k1[...] = jnp.zeros(k1.shape, k1.dtype)
        vt1[...] = jnp.zeros(vt1.shape, vt1.dtype)
        yc1[...] = jnp.zeros(yc1.shape, yc1.dtype)

    def step(cur, prev):
        q_c, k_c, vt_c, u_c, yc_c = cur
        q_p, k_p, vt_p, u_p, yc_p = prev

        @pl.when(tile_in == 0)
        def _():
            k_c[:, 0:WINDOW, :] = ck_ref[...]
            vt_c[:, 0:WINDOW] = cv_ref[...]
            u_c[0:CONV_CARRY_ROWS, :] = cu_ref[...]

        @pl.when(tile_in != 0)
        def _():
            k_c[:, 0:WINDOW, :] = k_p[:, tm:tm + WINDOW, :]
            vt_c[:, 0:WINDOW] = vt_p[:, tm:tm + WINDOW]
            u_c[0:CONV_CARRY_ROWS, :] = u_p[tm:tm + CONV_CARRY_ROWS, :]

        a_buf[...] = (_rms(h_in_ref[...]) * pre_g_ref[...]).astype(jnp.bfloat16)

        def w_in_chunk(c):
            cols = slice(c * W_IN_CHUNK, (c + 1) * W_IN_CHUNK)
            proj_buf[:, cols] = jnp.dot(a_buf[...], w_in_ref[:, cols],
                                        preferred_element_type=jnp.float32)

        q_of = functools.partial(_post_q, proj_buf=proj_buf, tab_ref=tab_ref, q_buf=q_c, tm=tm)
        conv_of = functools.partial(_post_conv, proj_buf=proj_buf, conv_w_ref=conv_w_ref,
                                    ou_ref=ou_ref, u_buf=u_c, ycf_buf=ycf_buf, tm=tm)

        def conv_tail():
            conv_of(1)
            _post_conv_norm(ycf_buf, conv_g_ref, yc_c)

        chunk_plan = [
            (0, lambda: q_of(0)), (1, lambda: q_of(1)),
            (5, None), (7, None), (3, lambda: conv_of(0)),
            (6, None), (8, None), (4, conv_tail),
            (2, lambda: _post_kv(proj_buf, tab_ref, k_c, vt_c, ok_ref, ov_ref, tm=tm)),
        ]
        assert sorted(c for c, _ in chunk_plan) == list(range(IN_WIDTH // W_IN_CHUNK))

        def run_chunk(entry):
            c, post = entry
            w_in_chunk(c)
            if post is not None:
                post()

        n_items = (tm // QB) * N_KV_HEADS
        first_out = tile_out == 0
        s_next = _scores(0, first_out, q_p, k_p, bias_buf)
        for item in range(n_items):
            s_cur = s_next
            if item + 1 < n_items:
                s_next = _scores(item + 1, first_out, q_p, k_p, bias_buf)
            if item < len(chunk_plan):
                run_chunk(chunk_plan[item])
            _softmax_pv(item, s_cur, layer, sinks_ref, vt_p, yat_buf)
        for entry in chunk_plan[n_items:]:
            run_chunk(entry)
        _mixer_out_post(h_out_ref, attn_g_ref, w_out_ref, post_g_ref, o_ref, yc_p, yat_buf)

    @pl.when(s % 2 == 0)
    def _():
        step(slots[0], slots[1])

    @pl.when(s % 2 == 1)
    def _():
        step(slots[1], slots[0])


def _mixer(h, tab, ck, cv, cu, cb, sinks, pre_g, w_in, conv_w, attn_g, conv_g, w_out, post_g,
           *, layer, w_layer, batch, tm):
    rows = h.shape[0]
    n_steps = rows // tm
    n_tiles = n_steps // batch
    const = lambda s: (0, 0)
    const3 = lambda s: (0, 0, 0)
    of_layer = lambda s: (layer, 0, 0)
    of_w_layer = lambda s: (w_layer, 0, 0)
    in_map = lambda s: (jnp.minimum(s, n_steps - 1), 0)
    out_map = lambda s: (jnp.maximum(s - 1, 0), 0)
    seq_of_in = lambda s: jnp.minimum(s, n_steps - 1) // n_tiles
    resident = dict(pipeline_mode=pl.Buffered(1))
    kernel = functools.partial(_mixer_kernel, layer=layer, tm=tm, n_tiles=n_tiles,
                               n_steps=n_steps)
    slot = [
        pltpu.VMEM((N_Q_HEADS, tm, HEAD_DIM), jnp.bfloat16),
        pltpu.VMEM((N_KV_HEADS, WINDOW + tm, HEAD_DIM), jnp.bfloat16),
        pltpu.VMEM((KV_WIDTH, WINDOW + tm), jnp.float32),
        pltpu.VMEM((CONV_CARRY_ROWS + tm, CONV_WIDTH), jnp.float32),
        pltpu.VMEM((tm, CONV_WIDTH), jnp.bfloat16),
    ]
    return pl.pallas_call(
        kernel,
        grid=(n_steps + 1,),
        in_specs=[
            pl.BlockSpec(memory_space=pltpu.SMEM),
            pl.BlockSpec((tm, D_MODEL), in_map),
            pl.BlockSpec((tm, D_MODEL), out_map),
            pl.BlockSpec((tm, 2 * LANES),
                         lambda s: (jnp.minimum(s, n_steps - 1) % n_tiles, 0)),
            pl.BlockSpec((N_KV_HEADS, WINDOW, HEAD_DIM), const3),
            pl.BlockSpec((KV_WIDTH, WINDOW), const),
            pl.BlockSpec((CONV_CARRY_ROWS, CONV_WIDTH), const),
            pl.BlockSpec((KB, 1), const),
            pl.BlockSpec((None, 1, D_MODEL), of_layer),
            pl.BlockSpec((None, D_MODEL, IN_WIDTH), of_w_layer, **resident),
            pl.BlockSpec((None, CONV_K, CONV_WIDTH), of_layer),
            pl.BlockSpec((None, ATTN_WIDTH, 1), of_layer),
            pl.BlockSpec((None, 1, CONV_WIDTH), of_layer),
            pl.BlockSpec((None, D_MODEL, D_MODEL), of_w_layer, **resident),
            pl.BlockSpec((None, 1, D_MODEL), of_layer),
        ],
        out_specs=[
            pl.BlockSpec((tm, D_MODEL), out_map),
            pl.BlockSpec((1, N_KV_HEADS, WINDOW, HEAD_DIM), lambda s: (seq_of_in(s), 0, 0, 0)),
            pl.BlockSpec((1, KV_WIDTH, WINDOW), lambda s: (seq_of_in(s), 0, 0)),
            pl.BlockSpec((1, CONV_CARRY_ROWS, CONV_WIDTH), lambda s: (seq_of_in(s), 0, 0)),
        ],
        out_shape=[
            jax.ShapeDtypeStruct((rows, D_MODEL), jnp.float32),
            jax.ShapeDtypeStruct((batch, N_KV_HEADS, WINDOW, HEAD_DIM), jnp.bfloat16),
            jax.ShapeDtypeStruct((batch, KV_WIDTH, WINDOW), jnp.float32),
            jax.ShapeDtypeStruct((batch, CONV_CARRY_ROWS, CONV_WIDTH), jnp.float32),
        ],
        scratch_shapes=slot + slot + [
            pltpu.VMEM((ATTN_WIDTH, tm), jnp.float32),
            pltpu.VMEM((2, KB, GROUP * QB), jnp.float32),
            pltpu.VMEM((tm, D_MODEL), jnp.bfloat16),
            pltpu.VMEM((tm, IN_WIDTH), jnp.float32),
            pltpu.VMEM((tm, CONV_WIDTH), jnp.float32),
        ],
        compiler_params=pltpu.CompilerParams(
            dimension_semantics=("arbitrary",),
            vmem_limit_bytes=VMEM_LIMIT_BYTES),
        name="mixer",
    )(sinks, h, h, tab, ck, cv, cu, cb, pre_g, w_in, conv_w, attn_g, conv_g, w_out, post_g)


def _mlp_kernel(*refs, tm, n_cast):
    h_ref, pre_g_ref, w_up_ref, w_down_ref, post_g_ref = refs[:5]
    cast_src = refs[5:5 + n_cast]
    o_ref = refs[5 + n_cast]
    cast_dst = refs[6 + n_cast:]
    for src, dst in zip(cast_src, cast_dst):
        dst[...] = src[...].astype(jnp.bfloat16)

    sub = min(MLP_SUB_ROWS, tm)
    subs = [slice(r, r + sub) for r in range(0, tm, sub)]
    a = [(_rms(h_ref[rows, :]) * pre_g_ref[...]).astype(jnp.bfloat16) for rows in subs]
    for a_sub, rows in zip(a, subs):
        acc = jnp.zeros((sub, D_MODEL), jnp.float32)
        for c in range(D_FF // FF_CHUNK):
            cols = slice(c * FF_CHUNK, (c + 1) * FF_CHUNK)
            up = jnp.dot(a_sub, w_up_ref[:, cols], preferred_element_type=jnp.float32)
            act = jnp.square(jnp.maximum(up, 0.0)).astype(jnp.bfloat16)
            acc = acc + jnp.dot(act, w_down_ref[cols, :], preferred_element_type=jnp.float32)
        o_ref[rows, :] = h_ref[rows, :] + _rms(acc) * post_g_ref[...]


def _mlp(h, pre_g, w_up, w_down, post_g, *, layer, w_layer, tm, cast=(), cast_layer=0):
    rows = h.shape[0]
    steps = rows // tm
    of_layer = lambda i: (layer, 0, 0)
    of_w_layer = lambda i: (w_layer, 0, 0)
    resident = dict(pipeline_mode=pl.Buffered(1))
    slab = lambda w: (None, w.shape[1] // steps, w.shape[2])
    outs = pl.pallas_call(
        functools.partial(_mlp_kernel, tm=tm, n_cast=len(cast)),
        grid=(steps,),
        in_specs=[
            pl.BlockSpec((tm, D_MODEL), lambda i: (i, 0)),
            pl.BlockSpec((None, 1, D_MODEL), of_layer),
            pl.BlockSpec((None, D_MODEL, D_FF), of_w_layer, **resident),
            pl.BlockSpec((None, D_FF, D_MODEL), of_w_layer, **resident),
            pl.BlockSpec((None, 1, D_MODEL), of_layer),
        ] + [pl.BlockSpec(slab(w), lambda i: (cast_layer, i, 0)) for w in cast],
        out_specs=[pl.BlockSpec((tm, D_MODEL), lambda i: (i, 0))]
        + [pl.BlockSpec(slab(w), lambda i: (0, i, 0)) for w in cast],
        out_shape=[jax.ShapeDtypeStruct((rows, D_MODEL), jnp.float32)]
        + [jax.ShapeDtypeStruct((1,) + w.shape[1:], jnp.bfloat16) for w in cast],
        compiler_params=pltpu.CompilerParams(
            dimension_semantics=("arbitrary",),
            vmem_limit_bytes=VMEM_LIMIT_BYTES),
        name="mlp",
    )(h, pre_g, w_up, w_down, post_g, *cast)
    return outs[0], list(outs[1:])


def _rope_table(first_pos, n_pos, lead_rows=0):
    pos = first_pos + np.arange(n_pos, dtype=np.float64)
    inv_freq = np.power(ROPE_THETA, -np.arange(0, ROT_DIM, 2, dtype=np.float64) / ROT_DIM)
    half = ROT_DIM // 2
    d = np.arange(LANES) % HEAD_DIM
    ang = pos[:, None] * inv_freq[d % half][None, :]
    c = np.where(d < ROT_DIM, np.cos(ang), 1.0)
    s = np.where(d < half, -np.sin(ang), np.where(d < ROT_DIM, np.sin(ang), 0.0))
    tab = np.concatenate([c, s], axis=1).astype(np.float32)
    return jnp.asarray(np.pad(tab, ((lead_rows, 0), (0, 0))))


def kernel(x, meta_tokens, mix_pre_g, w_in, conv_w, sinks, attn_out_g, conv_out_g, w_out,
           mix_post_g, mlp_pre_g, w_up, w_down, mlp_post_g):
    batch, seq, _ = x.shape
    depth = w_in.shape[0]
    bf16 = jnp.bfloat16
    f32 = jnp.float32
    lead = QB - N_META

    tab_meta = _rope_table(0, N_META, lead_rows=lead)
    tab_x = _rope_table(N_META, seq)

    hx = x.reshape(batch * seq, D_MODEL)
    hm = jnp.pad(meta_tokens.astype(x.dtype), ((lead, 0), (0, 0)))

    masked = np.full((WINDOW, 1), MASKED, np.float32)
    tail_valid = np.where(np.arange(WINDOW)[:, None] >= lead, 0.0, MASKED).astype(np.float32)
    cb_meta = jnp.asarray(np.concatenate([masked, tail_valid]))
    cb_x = jnp.asarray(np.concatenate([tail_valid, np.zeros((QB, 1), np.float32)]))
    zero_k = jnp.zeros((N_KV_HEADS, WINDOW, HEAD_DIM), bf16)
    zero_vt = jnp.zeros((KV_WIDTH, WINDOW), f32)
    zero_u = jnp.zeros((CONV_CARRY_ROWS, CONV_WIDTH), f32)

    rows_of = lambda v: v.reshape(depth, 1, -1)
    pre_g, post_g = rows_of(mix_pre_g), rows_of(mix_post_g)
    attn_g, conv_g = attn_out_g.reshape(depth, -1, 1), rows_of(conv_out_g)
    ff_pre_g, ff_post_g = rows_of(mlp_pre_g), rows_of(mlp_post_g)
    big = (w_in, w_out, w_up, w_down)
    w_in_b, w_out_b, w_up_b, w_down_b = (w[0:1].astype(bf16) for w in big)
    for l in range(depth):
        mixer_w = (sinks, pre_g, w_in_b, conv_w, attn_g, conv_g, w_out_b, post_g)
        mlp_w = (ff_pre_g, w_up_b, w_down_b, ff_post_g)
        hm, mk, mv, mu = _mixer(hm, tab_meta, zero_k, zero_vt, zero_u, cb_meta, *mixer_w,
                                layer=l, w_layer=0, batch=1, tm=QB)
        hx = _mixer(hx, tab_x, mk[0], mv[0], mu[0], cb_x, *mixer_w,
                    layer=l, w_layer=0, batch=batch, tm=ROW_TILE)[0]
        if l + 1 < depth:
            hm, _ = _mlp(hm, *mlp_w, layer=l, w_layer=0, tm=QB)
            hx, (w_in_b, w_out_b, w_up_b, w_down_b) = _mlp(
                hx, *mlp_w, layer=l, w_layer=0, tm=MLP_TILE, cast=big, cast_layer=l + 1)
        else:
            hx, _ = _mlp(hx, *mlp_w, layer=l, w_layer=0, tm=MLP_TILE)
    return hx.reshape(batch, seq, D_MODEL)
```

```python
import functools

import jax
import jax.numpy as jnp
import numpy as np
from jax import lax
from jax.experimental import pallas as pl
from jax.experimental.pallas import tpu as pltpu

D_MODEL = 1024
N_META = 16
ATTN_WIDTH = 512
CONV_WIDTH = 512
HEAD_DIM = 64
N_Q_HEADS = 8
N_KV_HEADS = 2
GROUP = N_Q_HEADS // N_KV_HEADS
KV_WIDTH = N_KV_HEADS * HEAD_DIM
CONV_K = 3
WINDOW = 128
ROPE_THETA = 500000.0
ROT_DIM = HEAD_DIM // 4
D_FF = 4 * D_MODEL
IN_WIDTH = ATTN_WIDTH + 2 * KV_WIDTH + 3 * CONV_WIDTH
EPS = 1e-6

S_Q = ATTN_WIDTH
S_K = S_Q + KV_WIDTH
S_V = S_K + KV_WIDTH
S_B = S_V + CONV_WIDTH
S_C = S_B + CONV_WIDTH

LANES = 128
QB = WINDOW
KB = WINDOW + QB
CONV_CARRY_ROWS = 8
MASKED = -1e30
LOG2_E = 1.4426950408889634
SCORE_SCALE_LOG2 = HEAD_DIM ** -0.5 * LOG2_E
ROW_TILE = 512
FF_CHUNK = 1024
MLP_TILE = 1024
MLP_SUB_ROWS = 512
W_IN_CHUNK = 256
VMEM_LIMIT_BYTES = 56 * 1024 * 1024


def _rms(x):
    return x * lax.rsqrt(jnp.mean(x * x, axis=-1, keepdims=True) + EPS)


def _rope(t, tab, low_half):
    half = ROT_DIM // 2
    partner = jnp.where(low_half,
                        pltpu.roll(t, LANES - half, axis=1),
                        pltpu.roll(t, half, axis=1))
    return t * tab[:, 0:LANES] + partner * tab[:, LANES:2 * LANES]


def _low_half(tm):
    return lax.broadcasted_iota(jnp.int32, (tm, LANES), 1) % HEAD_DIM < ROT_DIM // 2


def _post_q(chunk, proj_buf, tab_ref, q_buf, *, tm):
    tab = tab_ref[...]
    for c in range(chunk * W_IN_CHUNK // LANES, (chunk + 1) * W_IN_CHUNK // LANES):
        qc = _rope(proj_buf[:, c * LANES:(c + 1) * LANES], tab, _low_half(tm)
                   ).astype(jnp.bfloat16)
        q_buf[2 * c] = qc[:, :HEAD_DIM]
        q_buf[2 * c + 1] = qc[:, HEAD_DIM:]


def _post_kv(proj_buf, tab_ref, k_buf, vt_buf, ok_ref, ov_ref, *, tm):
    kr = (_rope(proj_buf[:, S_Q:S_K], tab_ref[...], _low_half(tm)) * SCORE_SCALE_LOG2
          ).astype(jnp.bfloat16)
    for g in range(N_KV_HEADS):
        kg = kr[:, g * HEAD_DIM:(g + 1) * HEAD_DIM]
        k_buf[g, WINDOW:WINDOW + tm, :] = kg
        ok_ref[0, g] = kg[tm - WINDOW:, :]
    vt = proj_buf[:, S_K:S_V].T
    vt_buf[:, WINDOW:WINDOW + tm] = vt
    ov_ref[0] = vt[:, tm - WINDOW:]


def _post_conv(half, proj_buf, conv_w_ref, ou_ref, u_buf, ycf_buf, *, tm):
    cols = slice(half * W_IN_CHUNK, (half + 1) * W_IN_CHUNK)
    gate = lambda start: proj_buf[:, start + half * W_IN_CHUNK:start + (half + 1) * W_IN_CHUNK]
    u = gate(S_B) * gate(S_C)
    u_buf[CONV_CARRY_ROWS:CONV_CARRY_ROWS + tm, cols] = u
    ou_ref[0, :, cols] = u[tm - CONV_CARRY_ROWS:, :]
    cw = conv_w_ref[:, cols]
    y_conv = cw[2:3, :] * u
    y_conv += cw[1:2, :] * u_buf[CONV_CARRY_ROWS - 1:CONV_CARRY_ROWS - 1 + tm, cols]
    y_conv += cw[0:1, :] * u_buf[CONV_CARRY_ROWS - 2:CONV_CARRY_ROWS - 2 + tm, cols]
    ycf_buf[:, cols] = gate(S_V) * y_conv


def _post_conv_norm(ycf_buf, conv_g_ref, yc_buf):
    yc_buf[...] = (_rms(ycf_buf[...]) * conv_g_ref[...]).astype(jnp.bfloat16)


def _scores(item, first_tile, q_buf, k_buf, bias_buf):
    j, g = divmod(item, N_KV_HEADS)
    r0 = j * QB
    bias = bias_buf[jnp.where(first_tile, 0, 1)] if j == 0 else bias_buf[1]
    kg = k_buf[g, r0:r0 + KB, :]
    qg = q_buf[g * GROUP:(g + 1) * GROUP, r0:r0 + QB, :].reshape(GROUP * QB, HEAD_DIM)
    return lax.dot_general(kg, qg, (((1,), (1,)), ((), ())),
                           preferred_element_type=jnp.float32) + bias


def _softmax_pv(item, s, layer, sinks_ref, vt_buf, yat_buf):
    j, g = divmod(item, N_KV_HEADS)
    r0 = j * QB
    sink = jnp.concatenate(
        [jnp.full((1, QB), sinks_ref[layer, g * GROUP + i] * LOG2_E, jnp.float32)
         for i in range(GROUP)], axis=1)
    m = jnp.maximum(jnp.max(s, axis=0, keepdims=True), sink)
    e = jnp.exp2(s - m)
    den = jnp.sum(e, axis=0, keepdims=True) + jnp.exp2(sink - m)
    vt = vt_buf[g * HEAD_DIM:(g + 1) * HEAD_DIM, r0:r0 + KB].astype(jnp.bfloat16)
    o = jnp.dot(vt, e.astype(jnp.bfloat16), preferred_element_type=jnp.float32)
    o = o * (1.0 / den)
    for i in range(GROUP):
        hd = g * GROUP + i
        yat_buf[hd * HEAD_DIM:(hd + 1) * HEAD_DIM, r0:r0 + QB] = o[:, i * QB:(i + 1) * QB]


def _mixer_out_post(h_ref, attn_g_ref, w_out_ref, post_g_ref, o_ref, yc_buf, yat_buf):
    yat = yat_buf[...]
    inv = lax.rsqrt(jnp.mean(yat * yat, axis=0, keepdims=True) + EPS)
    ya_n = (yat * inv * attn_g_ref[...]).astype(jnp.bfloat16)
    mixed = lax.dot_general(ya_n, w_out_ref[0:ATTN_WIDTH, :], (((0,), (0,)), ((), ())),
                            preferred_element_type=jnp.float32)
    mixed += jnp.dot(yc_buf[...], w_out_ref[ATTN_WIDTH:, :], preferred_element_type=jnp.float32)
    o_ref[...] = h_ref[...] + _rms(mixed) * post_g_ref[...]


N_MIXER_INPUTS = 15
N_MIXER_OUTPUTS = 4


def _mixer_kernel(*refs, layer, tm, n_tiles, n_steps, n_cast):
    (sinks_ref, h_in_ref, h_out_ref, tab_ref, ck_ref, cv_ref, cu_ref, cb_ref,
     pre_g_ref, w_in_ref, conv_w_ref, attn_g_ref, conv_g_ref, w_out_ref, post_g_ref
     ) = refs[:N_MIXER_INPUTS]
    cast_src = refs[N_MIXER_INPUTS:N_MIXER_INPUTS + n_cast]
    outs = refs[N_MIXER_INPUTS + n_cast:]
    o_ref, ok_ref, ov_ref, ou_ref = outs[:N_MIXER_OUTPUTS]
    cast_dst = outs[N_MIXER_OUTPUTS:N_MIXER_OUTPUTS + n_cast]
    (q0, k0, vt0, u0, yc0, q1, k1, vt1, u1, yc1, yat_buf, bias_buf, a_buf, proj_buf, ycf_buf
     ) = outs[N_MIXER_OUTPUTS + n_cast:]
    pre_g_ref, conv_g_ref, post_g_ref = (
        g.at[layer:layer + 1] for g in (pre_g_ref, conv_g_ref, post_g_ref))
    for src, dst in zip(cast_src, cast_dst):
        dst[...] = src[...].astype(jnp.bfloat16)
    s = pl.program_id(0)
    tile_in = jnp.minimum(s, n_steps - 1) % n_tiles
    tile_out = jnp.maximum(s - 1, 0) % n_tiles
    slots = ((q0, k0, vt0, u0, yc0), (q1, k1, vt1, u1, yc1))

    @pl.when(s == 0)
    def _():
        key_idx = lax.broadcasted_iota(jnp.int32, (KB, QB), 0)
        qry_idx = lax.broadcasted_iota(jnp.int32, (KB, QB), 1)
        band = jnp.where((key_idx > qry_idx) & (key_idx <= qry_idx + WINDOW), 0.0, MASKED)
        band = jnp.concatenate([band] * GROUP, axis=1)
        bias_buf[1] = band
        bias_buf[0] = band + cb_ref[...]
        q1[...] = jnp.zeros(q1.shape, q1.dtype)
        k1[...] = jnp.zeros(k1.shape, k1.dtype)
        vt1[...] = jnp.zeros(vt1.shape, vt1.dtype)
        yc1[...] = jnp.zeros(yc1.shape, yc1.dtype)

    def step(cur, prev):
        q_c, k_c, vt_c, u_c, yc_c = cur
        q_p, k_p, vt_p, u_p, yc_p = prev

        @pl.when(tile_in == 0)
        def _():
            k_c[:, 0:WINDOW, :] = ck_ref[...]
            vt_c[:, 0:WINDOW] = cv_ref[...]
            u_c[0:CONV_CARRY_ROWS, :] = cu_ref[...]

        @pl.when(tile_in != 0)
        def _():
            k_c[:, 0:WINDOW, :] = k_p[:, tm:tm + WINDOW, :]
            vt_c[:, 0:WINDOW] = vt_p[:, tm:tm + WINDOW]
            u_c[0:CONV_CARRY_ROWS, :] = u_p[tm:tm + CONV_CARRY_ROWS, :]

        a_buf[...] = (_rms(h_in_ref[...]) * pre_g_ref[...]).astype(jnp.bfloat16)

        def w_in_chunk(c):
            cols = slice(c * W_IN_CHUNK, (c + 1) * W_IN_CHUNK)
            proj_buf[:, cols] = jnp.dot(a_buf[...], w_in_ref[:, cols],
                                        preferred_element_type=jnp.float32)

        q_of = functools.partial(_post_q, proj_buf=proj_buf, tab_ref=tab_ref, q_buf=q_c, tm=tm)
        conv_of = functools.partial(_post_conv, proj_buf=proj_buf, conv_w_ref=conv_w_ref,
                                    ou_ref=ou_ref, u_buf=u_c, ycf_buf=ycf_buf, tm=tm)

        def conv_tail():
            conv_of(1)
            _post_conv_norm(ycf_buf, conv_g_ref, yc_c)

        chunk_plan = [
            (0, lambda: q_of(0)), (1, lambda: q_of(1)),
            (5, None), (7, None), (3, lambda: conv_of(0)),
            (6, None), (8, None), (4, conv_tail),
            (2, lambda: _post_kv(proj_buf, tab_ref, k_c, vt_c, ok_ref, ov_ref, tm=tm)),
        ]
        assert sorted(c for c, _ in chunk_plan) == list(range(IN_WIDTH // W_IN_CHUNK))

        def run_chunk(entry):
            c, post = entry
            w_in_chunk(c)
            if post is not None:
                post()

        n_items = (tm // QB) * N_KV_HEADS
        first_out = tile_out == 0
        s_next = _scores(0, first_out, q_p, k_p, bias_buf)
        for item in range(n_items):
            s_cur = s_next
            if item + 1 < n_items:
                s_next = _scores(item + 1, first_out, q_p, k_p, bias_buf)
            if item < len(chunk_plan):
                run_chunk(chunk_plan[item])
            _softmax_pv(item, s_cur, layer, sinks_ref, vt_p, yat_buf)
        for entry in chunk_plan[n_items:]:
            run_chunk(entry)
        _mixer_out_post(h_out_ref, attn_g_ref, w_out_ref, post_g_ref, o_ref, yc_p, yat_buf)

    @pl.when(s % 2 == 0)
    def _():
        step(slots[0], slots[1])

    @pl.when(s % 2 == 1)
    def _():
        step(slots[1], slots[0])


def _mixer(h, tab, ck, cv, cu, cb, sinks, pre_g, w_in, conv_w, attn_g, conv_g, w_out, post_g,
           *, layer, w_layer, batch, tm, cast=(), cast_layer=0):
    rows = h.shape[0]
    n_steps = rows // tm
    n_tiles = n_steps // batch
    const = lambda s: (0, 0)
    const3 = lambda s: (0, 0, 0)
    of_layer = lambda s: (layer, 0, 0)
    of_w_layer = lambda s: (w_layer, 0, 0)
    in_map = lambda s: (jnp.minimum(s, n_steps - 1), 0)
    out_map = lambda s: (jnp.maximum(s - 1, 0), 0)
    seq_of_in = lambda s: jnp.minimum(s, n_steps - 1) // n_tiles
    resident = dict(pipeline_mode=pl.Buffered(1))
    depth = pre_g.shape[0]
    slab = lambda w: (None, w.shape[1] // n_steps, w.shape[2])
    kernel = functools.partial(_mixer_kernel, layer=layer, tm=tm, n_tiles=n_tiles,
                               n_steps=n_steps, n_cast=len(cast))
    slot = [
        pltpu.VMEM((N_Q_HEADS, tm, HEAD_DIM), jnp.bfloat16),
        pltpu.VMEM((N_KV_HEADS, WINDOW + tm, HEAD_DIM), jnp.bfloat16),
        pltpu.VMEM((KV_WIDTH, WINDOW + tm), jnp.float32),
        pltpu.VMEM((CONV_CARRY_ROWS + tm, CONV_WIDTH), jnp.float32),
        pltpu.VMEM((tm, CONV_WIDTH), jnp.bfloat16),
    ]
    return pl.pallas_call(
        kernel,
        grid=(n_steps + 1,),
        in_specs=[
            pl.BlockSpec(memory_space=pltpu.SMEM),
            pl.BlockSpec((tm, D_MODEL), in_map),
            pl.BlockSpec((tm, D_MODEL), out_map),
            pl.BlockSpec((tm, 2 * LANES),
                         lambda s: (jnp.minimum(s, n_steps - 1) % n_tiles, 0)),
            pl.BlockSpec((N_KV_HEADS, WINDOW, HEAD_DIM), const3),
            pl.BlockSpec((KV_WIDTH, WINDOW), const),
            pl.BlockSpec((CONV_CARRY_ROWS, CONV_WIDTH), const),
            pl.BlockSpec((KB, 1), const),
            pl.BlockSpec((depth, D_MODEL), const),
            pl.BlockSpec((None, D_MODEL, IN_WIDTH), of_w_layer, **resident),
            pl.BlockSpec((None, CONV_K, CONV_WIDTH), of_layer),
            pl.BlockSpec((None, ATTN_WIDTH, 1), of_layer),
            pl.BlockSpec((depth, CONV_WIDTH), const),
            pl.BlockSpec((None, D_MODEL, D_MODEL), of_w_layer, **resident),
            pl.BlockSpec((depth, D_MODEL), const),
        ] + [pl.BlockSpec(slab(w), lambda s: (cast_layer, jnp.minimum(s, n_steps - 1), 0))
             for w in cast],
        out_specs=[
            pl.BlockSpec((tm, D_MODEL), out_map),
            pl.BlockSpec((1, N_KV_HEADS, WINDOW, HEAD_DIM), lambda s: (seq_of_in(s), 0, 0, 0)),
            pl.BlockSpec((1, KV_WIDTH, WINDOW), lambda s: (seq_of_in(s), 0, 0)),
            pl.BlockSpec((1, CONV_CARRY_ROWS, CONV_WIDTH), lambda s: (seq_of_in(s), 0, 0)),
        ] + [pl.BlockSpec(slab(w), lambda s: (0, jnp.minimum(s, n_steps - 1), 0)) for w in cast],
        out_shape=[
            jax.ShapeDtypeStruct((rows, D_MODEL), jnp.float32),
            jax.ShapeDtypeStruct((batch, N_KV_HEADS, WINDOW, HEAD_DIM), jnp.bfloat16),
            jax.ShapeDtypeStruct((batch, KV_WIDTH, WINDOW), jnp.float32),
            jax.ShapeDtypeStruct((batch, CONV_CARRY_ROWS, CONV_WIDTH), jnp.float32),
        ] + [jax.ShapeDtypeStruct((1,) + w.shape[1:], jnp.bfloat16) for w in cast],
        scratch_shapes=slot + slot + [
            pltpu.VMEM((ATTN_WIDTH, tm), jnp.float32),
            pltpu.VMEM((2, KB, GROUP * QB), jnp.float32),
            pltpu.VMEM((tm, D_MODEL), jnp.bfloat16),
            pltpu.VMEM((tm, IN_WIDTH), jnp.float32),
            pltpu.VMEM((tm, CONV_WIDTH), jnp.float32),
        ],
        compiler_params=pltpu.CompilerParams(
            dimension_semantics=("arbitrary",),
            vmem_limit_bytes=VMEM_LIMIT_BYTES),
        name="mixer",
    )(sinks, h, h, tab, ck, cv, cu, cb, pre_g, w_in, conv_w, attn_g, conv_g, w_out, post_g, *cast)


def _mlp_kernel(*refs, layer, tm, n_cast):
    h_ref, pre_g_ref, w_up_ref, w_down_ref, post_g_ref = refs[:5]
    pre_g_ref, post_g_ref = (g.at[layer:layer + 1] for g in (pre_g_ref, post_g_ref))
    cast_src = refs[5:5 + n_cast]
    o_ref = refs[5 + n_cast]
    cast_dst = refs[6 + n_cast:]
    for src, dst in zip(cast_src, cast_dst):
        dst[...] = src[...].astype(jnp.bfloat16)

    sub = min(MLP_SUB_ROWS, tm)
    subs = [slice(r, r + sub) for r in range(0, tm, sub)]
    a = [(_rms(h_ref[rows, :]) * pre_g_ref[...]).astype(jnp.bfloat16) for rows in subs]
    for a_sub, rows in zip(a, subs):
        acc = jnp.zeros((sub, D_MODEL), jnp.float32)
        for c in range(D_FF // FF_CHUNK):
            cols = slice(c * FF_CHUNK, (c + 1) * FF_CHUNK)
            up = jnp.dot(a_sub, w_up_ref[:, cols], preferred_element_type=jnp.float32)
            act = jnp.square(jnp.maximum(up, 0.0)).astype(jnp.bfloat16)
            acc = acc + jnp.dot(act, w_down_ref[cols, :], preferred_element_type=jnp.float32)
        o_ref[rows, :] = h_ref[rows, :] + _rms(acc) * post_g_ref[...]


def _mlp(h, pre_g, w_up, w_down, post_g, *, layer, w_layer, tm, cast=(), cast_layer=0):
    rows = h.shape[0]
    steps = rows // tm
    depth = pre_g.shape[0]
    const = lambda i: (0, 0)
    of_w_layer = lambda i: (w_layer, 0, 0)
    resident = dict(pipeline_mode=pl.Buffered(1))
    slab = lambda w: (None, w.shape[1] // steps, w.shape[2])
    outs = pl.pallas_call(
        functools.partial(_mlp_kernel, layer=layer, tm=tm, n_cast=len(cast)),
        grid=(steps,),
        in_specs=[
            pl.BlockSpec((tm, D_MODEL), lambda i: (i, 0)),
            pl.BlockSpec((depth, D_MODEL), const),
            pl.BlockSpec((None, D_MODEL, D_FF), of_w_layer, **resident),
            pl.BlockSpec((None, D_FF, D_MODEL), of_w_layer, **resident),
            pl.BlockSpec((depth, D_MODEL), const),
        ] + [pl.BlockSpec(slab(w), lambda i: (cast_layer, i, 0)) for w in cast],
        out_specs=[pl.BlockSpec((tm, D_MODEL), lambda i: (i, 0))]
        + [pl.BlockSpec(slab(w), lambda i: (0, i, 0)) for w in cast],
        out_shape=[jax.ShapeDtypeStruct((rows, D_MODEL), jnp.float32)]
        + [jax.ShapeDtypeStruct((1,) + w.shape[1:], jnp.bfloat16) for w in cast],
        compiler_params=pltpu.CompilerParams(
            dimension_semantics=("arbitrary",),
            vmem_limit_bytes=VMEM_LIMIT_BYTES),
        name="mlp",
    )(h, pre_g, w_up, w_down, post_g, *cast)
    return outs[0], list(outs[1:])


def _rope_table(first_pos, n_pos, lead_rows=0):
    pos = first_pos + np.arange(n_pos, dtype=np.float64)
    inv_freq = np.power(ROPE_THETA, -np.arange(0, ROT_DIM, 2, dtype=np.float64) / ROT_DIM)
    half = ROT_DIM // 2
    d = np.arange(LANES) % HEAD_DIM
    ang = pos[:, None] * inv_freq[d % half][None, :]
    c = np.where(d < ROT_DIM, np.cos(ang), 1.0)
    s = np.where(d < half, -np.sin(ang), np.where(d < ROT_DIM, np.sin(ang), 0.0))
    tab = np.concatenate([c, s], axis=1).astype(np.float32)
    return jnp.asarray(np.pad(tab, ((lead_rows, 0), (0, 0))))


def kernel(x, meta_tokens, mix_pre_g, w_in, conv_w, sinks, attn_out_g, conv_out_g, w_out,
           mix_post_g, mlp_pre_g, w_up, w_down, mlp_post_g):
    batch, seq, _ = x.shape
    depth = w_in.shape[0]
    bf16 = jnp.bfloat16
    f32 = jnp.float32
    lead = QB - N_META

    tab_meta = _rope_table(0, N_META, lead_rows=lead)
    tab_x = _rope_table(N_META, seq)

    hx = x.reshape(batch * seq, D_MODEL)
    hm = jnp.pad(meta_tokens.astype(x.dtype), ((lead, 0), (0, 0)))

    masked = np.full((WINDOW, 1), MASKED, np.float32)
    tail_valid = np.where(np.arange(WINDOW)[:, None] >= lead, 0.0, MASKED).astype(np.float32)
    cb_meta = jnp.asarray(np.concatenate([masked, tail_valid]))
    cb_x = jnp.asarray(np.concatenate([tail_valid, np.zeros((QB, 1), np.float32)]))
    zero_k = jnp.zeros((N_KV_HEADS, WINDOW, HEAD_DIM), bf16)
    zero_vt = jnp.zeros((KV_WIDTH, WINDOW), f32)
    zero_u = jnp.zeros((CONV_CARRY_ROWS, CONV_WIDTH), f32)

    attn_g = attn_out_g.reshape(depth, -1, 1)
    w_in_b, w_out_b = w_in[0:1].astype(bf16), w_out[0:1].astype(bf16)
    for l in range(depth):
        mixer_w = (sinks, mix_pre_g, w_in_b, conv_w, attn_g, conv_out_g, w_out_b, mix_post_g)
        hm, mk, mv, mu = _mixer(hm, tab_meta, zero_k, zero_vt, zero_u, cb_meta, *mixer_w,
                                layer=l, w_layer=0, batch=1, tm=QB)
        hx, _, _, _, w_up_b, w_down_b = _mixer(
            hx, tab_x, mk[0], mv[0], mu[0], cb_x, *mixer_w, layer=l, w_layer=0, batch=batch,
            tm=ROW_TILE, cast=(w_up, w_down), cast_layer=l)
        mlp_w = (mlp_pre_g, w_up_b, w_down_b, mlp_post_g)
        if l + 1 < depth:
            hm, _ = _mlp(hm, *mlp_w, layer=l, w_layer=0, tm=QB)
            hx, (w_in_b, w_out_b) = _mlp(hx, *mlp_w, layer=l, w_layer=0, tm=MLP_TILE,
                                         cast=(w_in, w_out), cast_layer=l + 1)
        else:
            hx, _ = _mlp(hx, *mlp_w, layer=l, w_layer=0, tm=MLP_TILE)
    return hx.reshape(batch, seq, D_MODEL)
```

```python
import functools

import jax
import jax.numpy as jnp
import numpy as np
from jax import lax
from jax.experimental import pallas as pl
from jax.experimental.pallas import tpu as pltpu

D_MODEL = 1024
N_META = 16
ATTN_WIDTH = 512
CONV_WIDTH = 512
HEAD_DIM = 64
N_Q_HEADS = 8
N_KV_HEADS = 2
GROUP = N_Q_HEADS // N_KV_HEADS
KV_WIDTH = N_KV_HEADS * HEAD_DIM
CONV_K = 3
WINDOW = 128
ROPE_THETA = 500000.0
ROT_DIM = HEAD_DIM // 4
D_FF = 4 * D_MODEL
IN_WIDTH = ATTN_WIDTH + 2 * KV_WIDTH + 3 * CONV_WIDTH
EPS = 1e-6

S_Q = ATTN_WIDTH
S_K = S_Q + KV_WIDTH
S_V = S_K + KV_WIDTH
S_B = S_V + CONV_WIDTH
S_C = S_B + CONV_WIDTH

LANES = 128
QB = WINDOW
KB = WINDOW + QB
CONV_CARRY_ROWS = 8
MASKED = -1e30
LOG2_E = 1.4426950408889634
SCORE_SCALE_LOG2 = HEAD_DIM ** -0.5 * LOG2_E
ROW_TILE = 512
FF_CHUNK = 1024
MLP_TILE = 512
NORM_ROWS = 32
W_IN_CHUNK = 256
VMEM_LIMIT_BYTES = 56 * 1024 * 1024


def _rms(x):
    return x * lax.rsqrt(jnp.mean(x * x, axis=-1, keepdims=True) + EPS)


def _exact_zero(vregs):
    bits = lax.bitcast_convert_type(vregs[0], jnp.uint32)
    for v in vregs[1:]:
        bits = bits | lax.bitcast_convert_type(v, jnp.uint32)
    sixteen = jnp.uint32(16)
    zero = lax.shift_right_logical(lax.shift_right_logical(bits, sixteen), sixteen)
    return lax.bitcast_convert_type(zero, jnp.float32)


def _rope(t, tab, low_half):
    half = ROT_DIM // 2
    partner = jnp.where(low_half,
                        pltpu.roll(t, LANES - half, axis=1),
                        pltpu.roll(t, half, axis=1))
    return t * tab[:, 0:LANES] + partner * tab[:, LANES:2 * LANES]


def _low_half(tm):
    return lax.broadcasted_iota(jnp.int32, (tm, LANES), 1) % HEAD_DIM < ROT_DIM // 2


def _post_q(chunk, proj_buf, tab_ref, q_buf, *, tm):
    tab = tab_ref[...]
    for c in range(chunk * W_IN_CHUNK // LANES, (chunk + 1) * W_IN_CHUNK // LANES):
        qc = _rope(proj_buf[:, c * LANES:(c + 1) * LANES], tab, _low_half(tm)
                   ).astype(jnp.bfloat16)
        q_buf[2 * c] = qc[:, :HEAD_DIM]
        q_buf[2 * c + 1] = qc[:, HEAD_DIM:]


def _post_kv(proj_buf, tab_ref, k_buf, vt_buf, ok_ref, ov_ref, *, tm):
    kr = (_rope(proj_buf[:, S_Q:S_K], tab_ref[...], _low_half(tm)) * SCORE_SCALE_LOG2
          ).astype(jnp.bfloat16)
    for g in range(N_KV_HEADS):
        kg = kr[:, g * HEAD_DIM:(g + 1) * HEAD_DIM]
        k_buf[g, WINDOW:WINDOW + tm, :] = kg
        ok_ref[0, g] = kg[tm - WINDOW:, :]
    vt = proj_buf[:, S_K:S_V].T
    vt_buf[:, WINDOW:WINDOW + tm] = vt
    ov_ref[0] = vt[:, tm - WINDOW:]


def _post_conv(half, proj_buf, conv_w_ref, ou_ref, u_buf, ycf_buf, *, tm):
    cols = slice(half * W_IN_CHUNK, (half + 1) * W_IN_CHUNK)
    gate = lambda start: proj_buf[:, start + half * W_IN_CHUNK:start + (half + 1) * W_IN_CHUNK]
    u = gate(S_B) * gate(S_C)
    u_buf[CONV_CARRY_ROWS:CONV_CARRY_ROWS + tm, cols] = u
    ou_ref[0, :, cols] = u[tm - CONV_CARRY_ROWS:, :]
    cw = conv_w_ref[:, cols]
    y_conv = cw[2:3, :] * u
    y_conv += cw[1:2, :] * u_buf[CONV_CARRY_ROWS - 1:CONV_CARRY_ROWS - 1 + tm, cols]
    y_conv += cw[0:1, :] * u_buf[CONV_CARRY_ROWS - 2:CONV_CARRY_ROWS - 2 + tm, cols]
    ycf_buf[:, cols] = gate(S_V) * y_conv


def _post_conv_norm(ycf_buf, conv_g_ref, yc_buf):
    yc_buf[...] = (_rms(ycf_buf[...]) * conv_g_ref[...]).astype(jnp.bfloat16)


def _scores(item, first_tile, q_buf, k_buf, bias_buf):
    j, g = divmod(item, N_KV_HEADS)
    r0 = j * QB
    bias = bias_buf[jnp.where(first_tile, 0, 1)] if j == 0 else bias_buf[1]
    kg = k_buf[g, r0:r0 + KB, :]
    qg = q_buf[g * GROUP:(g + 1) * GROUP, r0:r0 + QB, :].reshape(GROUP * QB, HEAD_DIM)
    return lax.dot_general(kg, qg, (((1,), (1,)), ((), ())),
                           preferred_element_type=jnp.float32) + bias


def _softmax_pv(item, s, layer, sinks_ref, vt_buf, yat_buf):
    j, g = divmod(item, N_KV_HEADS)
    r0 = j * QB
    sink = jnp.concatenate(
        [jnp.full((1, QB), sinks_ref[layer, g * GROUP + i] * LOG2_E, jnp.float32)
         for i in range(GROUP)], axis=1)
    m = jnp.maximum(jnp.max(s, axis=0, keepdims=True), sink)
    e = jnp.exp2(s - m)
    den = jnp.sum(e, axis=0, keepdims=True) + jnp.exp2(sink - m)
    vt = vt_buf[g * HEAD_DIM:(g + 1) * HEAD_DIM, r0:r0 + KB].astype(jnp.bfloat16)
    o = jnp.dot(vt, e.astype(jnp.bfloat16), preferred_element_type=jnp.float32)
    o = o * (1.0 / den)
    for i in range(GROUP):
        hd = g * GROUP + i
        yat_buf[hd * HEAD_DIM:(hd + 1) * HEAD_DIM, r0:r0 + QB] = o[:, i * QB:(i + 1) * QB]


def _mixer_out_post(h_ref, attn_g_ref, w_out_ref, post_g_ref, o_ref, yc_buf, yat_buf):
    yat = yat_buf[...]
    inv = lax.rsqrt(jnp.mean(yat * yat, axis=0, keepdims=True) + EPS)
    ya_n = (yat * inv * attn_g_ref[...]).astype(jnp.bfloat16)
    mixed = lax.dot_general(ya_n, w_out_ref[0:ATTN_WIDTH, :], (((0,), (0,)), ((), ())),
                            preferred_element_type=jnp.float32)
    mixed += jnp.dot(yc_buf[...], w_out_ref[ATTN_WIDTH:, :], preferred_element_type=jnp.float32)
    o_ref[...] = h_ref[...] + _rms(mixed) * post_g_ref[...]


N_MIXER_INPUTS = 15
N_MIXER_OUTPUTS = 4


def _mixer_kernel(*refs, layer, tm, n_tiles, n_steps, n_cast):
    (sinks_ref, h_in_ref, h_out_ref, tab_ref, ck_ref, cv_ref, cu_ref, cb_ref,
     pre_g_ref, w_in_ref, conv_w_ref, attn_g_ref, conv_g_ref, w_out_ref, post_g_ref
     ) = refs[:N_MIXER_INPUTS]
    cast_src = refs[N_MIXER_INPUTS:N_MIXER_INPUTS + n_cast]
    outs = refs[N_MIXER_INPUTS + n_cast:]
    o_ref, ok_ref, ov_ref, ou_ref = outs[:N_MIXER_OUTPUTS]
    cast_dst = outs[N_MIXER_OUTPUTS:N_MIXER_OUTPUTS + n_cast]
    (q0, k0, vt0, u0, yc0, q1, k1, vt1, u1, yc1, yat_buf, bias_buf, a_buf, proj_buf, ycf_buf
     ) = outs[N_MIXER_OUTPUTS + n_cast:]
    pre_g_ref, conv_g_ref, post_g_ref = (
        g.at[layer:layer + 1] for g in (pre_g_ref, conv_g_ref, post_g_ref))
    for src, dst in zip(cast_src, cast_dst):
        dst[...] = src[...].astype(jnp.bfloat16)
    s = pl.program_id(0)
    tile_in = jnp.minimum(s, n_steps - 1) % n_tiles
    tile_out = jnp.maximum(s - 1, 0) % n_tiles
    slots = ((q0, k0, vt0, u0, yc0), (q1, k1, vt1, u1, yc1))

    @pl.when(s == 0)
    def _():
        key_idx = lax.broadcasted_iota(jnp.int32, (KB, QB), 0)
        qry_idx = lax.broadcasted_iota(jnp.int32, (KB, QB), 1)
        band = jnp.where((key_idx > qry_idx) & (key_idx <= qry_idx + WINDOW), 0.0, MASKED)
        band = jnp.concatenate([band] * GROUP, axis=1)
        bias_buf[1] = band
        bias_buf[0] = band + cb_ref[...]
        q1[...] = jnp.zeros(q1.shape, q1.dtype)
        k1[...] = jnp.zeros(k1.shape, k1.dtype)
        vt1[...] = jnp.zeros(vt1.shape, vt1.dtype)
        yc1[...] = jnp.zeros(yc1.shape, yc1.dtype)

    def step(cur, prev):
        q_c, k_c, vt_c, u_c, yc_c = cur
        q_p, k_p, vt_p, u_p, yc_p = prev

        @pl.when(tile_in == 0)
        def _():
            k_c[:, 0:WINDOW, :] = ck_ref[...]
            vt_c[:, 0:WINDOW] = cv_ref[...]
            u_c[0:CONV_CARRY_ROWS, :] = cu_ref[...]

        @pl.when(tile_in != 0)
        def _():
            k_c[:, 0:WINDOW, :] = k_p[:, tm:tm + WINDOW, :]
            vt_c[:, 0:WINDOW] = vt_p[:, tm:tm + WINDOW]
            u_c[0:CONV_CARRY_ROWS, :] = u_p[tm:tm + CONV_CARRY_ROWS, :]

        a_buf[...] = (_rms(h_in_ref[...]) * pre_g_ref[...]).astype(jnp.bfloat16)

        def w_in_chunk(c):
            cols = slice(c * W_IN_CHUNK, (c + 1) * W_IN_CHUNK)
            proj_buf[:, cols] = jnp.dot(a_buf[...], w_in_ref[:, cols],
                                        preferred_element_type=jnp.float32)

        q_of = functools.partial(_post_q, proj_buf=proj_buf, tab_ref=tab_ref, q_buf=q_c, tm=tm)
        conv_of = functools.partial(_post_conv, proj_buf=proj_buf, conv_w_ref=conv_w_ref,
                                    ou_ref=ou_ref, u_buf=u_c, ycf_buf=ycf_buf, tm=tm)

        def conv_tail():
            conv_of(1)
            _post_conv_norm(ycf_buf, conv_g_ref, yc_c)

        chunk_plan = [
            (0, lambda: q_of(0)), (1, lambda: q_of(1)),
            (5, None), (7, None), (3, lambda: conv_of(0)),
            (6, None), (8, None), (4, conv_tail),
            (2, lambda: _post_kv(proj_buf, tab_ref, k_c, vt_c, ok_ref, ov_ref, tm=tm)),
        ]
        assert sorted(c for c, _ in chunk_plan) == list(range(IN_WIDTH // W_IN_CHUNK))

        def run_chunk(entry):
            c, post = entry
            w_in_chunk(c)
            if post is not None:
                post()

        n_items = (tm // QB) * N_KV_HEADS
        first_out = tile_out == 0
        s_next = _scores(0, first_out, q_p, k_p, bias_buf)
        for item in range(n_items):
            s_cur = s_next
            if item + 1 < n_items:
                s_next = _scores(item + 1, first_out, q_p, k_p, bias_buf)
            if item < len(chunk_plan):
                run_chunk(chunk_plan[item])
            _softmax_pv(item, s_cur, layer, sinks_ref, vt_p, yat_buf)
        for entry in chunk_plan[n_items:]:
            run_chunk(entry)
        _mixer_out_post(h_out_ref, attn_g_ref, w_out_ref, post_g_ref, o_ref, yc_p, yat_buf)

    @pl.when(s % 2 == 0)
    def _():
        step(slots[0], slots[1])

    @pl.when(s % 2 == 1)
    def _():
        step(slots[1], slots[0])


def _mixer(h, tab, ck, cv, cu, cb, sinks, pre_g, w_in, conv_w, attn_g, conv_g, w_out, post_g,
           *, layer, w_layer, batch, tm, cast=(), cast_layer=0):
    rows = h.shape[0]
    n_steps = rows // tm
    n_tiles = n_steps // batch
    const = lambda s: (0, 0)
    const3 = lambda s: (0, 0, 0)
    of_layer = lambda s: (layer, 0, 0)
    of_w_layer = lambda s: (w_layer, 0, 0)
    in_map = lambda s: (jnp.minimum(s, n_steps - 1), 0)
    out_map = lambda s: (jnp.maximum(s - 1, 0), 0)
    seq_of_in = lambda s: jnp.minimum(s, n_steps - 1) // n_tiles
    resident = dict(pipeline_mode=pl.Buffered(1))
    depth = pre_g.shape[0]
    slab = lambda w: (None, w.shape[1] // n_steps, w.shape[2])
    kernel = functools.partial(_mixer_kernel, layer=layer, tm=tm, n_tiles=n_tiles,
                               n_steps=n_steps, n_cast=len(cast))
    slot = [
        pltpu.VMEM((N_Q_HEADS, tm, HEAD_DIM), jnp.bfloat16),
        pltpu.VMEM((N_KV_HEADS, WINDOW + tm, HEAD_DIM), jnp.bfloat16),
        pltpu.VMEM((KV_WIDTH, WINDOW + tm), jnp.float32),
        pltpu.VMEM((CONV_CARRY_ROWS + tm, CONV_WIDTH), jnp.float32),
        pltpu.VMEM((tm, CONV_WIDTH), jnp.bfloat16),
    ]
    return pl.pallas_call(
        kernel,
        grid=(n_steps + 1,),
        in_specs=[
            pl.BlockSpec(memory_space=pltpu.SMEM),
            pl.BlockSpec((tm, D_MODEL), in_map),
            pl.BlockSpec((tm, D_MODEL), out_map),
            pl.BlockSpec((tm, 2 * LANES),
                         lambda s: (jnp.minimum(s, n_steps - 1) % n_tiles, 0)),
            pl.BlockSpec((N_KV_HEADS, WINDOW, HEAD_DIM), const3),
            pl.BlockSpec((KV_WIDTH, WINDOW), const),
            pl.BlockSpec((CONV_CARRY_ROWS, CONV_WIDTH), const),
            pl.BlockSpec((KB, 1), const),
            pl.BlockSpec((depth, D_MODEL), const),
            pl.BlockSpec((None, D_MODEL, IN_WIDTH), of_w_layer, **resident),
            pl.BlockSpec((None, CONV_K, CONV_WIDTH), of_layer),
            pl.BlockSpec((None, ATTN_WIDTH, 1), of_layer),
            pl.BlockSpec((depth, CONV_WIDTH), const),
            pl.BlockSpec((None, D_MODEL, D_MODEL), of_w_layer, **resident),
            pl.BlockSpec((depth, D_MODEL), const),
        ] + [pl.BlockSpec(slab(w), lambda s: (cast_layer, jnp.minimum(s, n_steps - 1), 0))
             for w in cast],
        out_specs=[
            pl.BlockSpec((tm, D_MODEL), out_map),
            pl.BlockSpec((1, N_KV_HEADS, WINDOW, HEAD_DIM), lambda s: (seq_of_in(s), 0, 0, 0)),
            pl.BlockSpec((1, KV_WIDTH, WINDOW), lambda s: (seq_of_in(s), 0, 0)),
            pl.BlockSpec((1, CONV_CARRY_ROWS, CONV_WIDTH), lambda s: (seq_of_in(s), 0, 0)),
        ] + [pl.BlockSpec(slab(w), lambda s: (0, jnp.minimum(s, n_steps - 1), 0)) for w in cast],
        out_shape=[
            jax.ShapeDtypeStruct((rows, D_MODEL), jnp.float32),
            jax.ShapeDtypeStruct((batch, N_KV_HEADS, WINDOW, HEAD_DIM), jnp.bfloat16),
            jax.ShapeDtypeStruct((batch, KV_WIDTH, WINDOW), jnp.float32),
            jax.ShapeDtypeStruct((batch, CONV_CARRY_ROWS, CONV_WIDTH), jnp.float32),
        ] + [jax.ShapeDtypeStruct((1,) + w.shape[1:], jnp.bfloat16) for w in cast],
        scratch_shapes=slot + slot + [
            pltpu.VMEM((ATTN_WIDTH, tm), jnp.float32),
            pltpu.VMEM((2, KB, GROUP * QB), jnp.float32),
            pltpu.VMEM((tm, D_MODEL), jnp.bfloat16),
            pltpu.VMEM((tm, IN_WIDTH), jnp.float32),
            pltpu.VMEM((tm, CONV_WIDTH), jnp.float32),
        ],
        compiler_params=pltpu.CompilerParams(
            dimension_semantics=("arbitrary",),
            vmem_limit_bytes=VMEM_LIMIT_BYTES),
        name="mixer",
    )(sinks, h, h, tab, ck, cv, cu, cb, pre_g, w_in, conv_w, attn_g, conv_g, w_out, post_g, *cast)


N_MLP_INPUTS = 6


def _mlp_kernel(*refs, layer, tm, n_cast):
    h_next_ref, h_prev_ref, pre_g_ref, w_up_ref, w_down_ref, post_g_ref = refs[:N_MLP_INPUTS]
    pre_g_ref, post_g_ref = (g.at[layer:layer + 1] for g in (pre_g_ref, post_g_ref))
    cast_src = refs[N_MLP_INPUTS:N_MLP_INPUTS + n_cast]
    outs = refs[N_MLP_INPUTS + n_cast:]
    o_ref = outs[0]
    cast_dst = outs[1:1 + n_cast]
    a0, a1, acc0, acc1 = outs[1 + n_cast:]
    for src, dst in zip(cast_src, cast_dst):
        dst[...] = src[...].astype(jnp.bfloat16)
    i = pl.program_id(0)

    def input_norm(h_ref, a_buf):
        seen = []
        for r in range(0, tm, NORM_ROWS):
            rows = slice(r, r + NORM_ROWS)
            a = _rms(h_ref[rows, :]) * pre_g_ref[...]
            a_buf[rows, :] = a.astype(jnp.bfloat16)
            seen.append(a[0:8, 0:LANES])
        return seen

    def finish(acc_buf):
        seen = []
        for r in range(0, tm, NORM_ROWS):
            rows = slice(r, r + NORM_ROWS)
            out = h_prev_ref[rows, :] + _rms(acc_buf[rows, :]) * post_g_ref[...]
            o_ref[rows, :] = out
            seen.append(out[0:8, 0:LANES])
        return seen

    @pl.when(i == 0)
    def _():
        input_norm(h_prev_ref, a0)
        acc1[...] = jnp.zeros(acc1.shape, acc1.dtype)

    def step(a_cur, a_nxt, acc_cur, acc_prev):
        def tie(up, seen):
            pieces = []
            for k, v in enumerate(seen):
                blk = up[k * NORM_ROWS:(k + 1) * NORM_ROWS]
                top = blk[0:8]
                top = jnp.concatenate([top[:, 0:LANES] + _exact_zero([v]), top[:, LANES:]], axis=1)
                pieces += [top, blk[8:]]
            return jnp.concatenate(pieces, axis=0)

        acc = jnp.zeros((tm, D_MODEL), jnp.float32)
        seen = None
        for c in range(D_FF // FF_CHUNK):
            cols = slice(c * FF_CHUNK, (c + 1) * FF_CHUNK)
            up = jnp.dot(a_cur[...], w_up_ref[:, cols], preferred_element_type=jnp.float32)
            if seen is not None:
                up = tie(up, seen)
            act = jnp.square(jnp.maximum(up, 0.0)).astype(jnp.bfloat16)
            acc = acc + jnp.dot(act, w_down_ref[cols, :], preferred_element_type=jnp.float32)
            if c == 0:
                seen = finish(acc_prev)
            elif c == 1:
                seen = input_norm(h_next_ref, a_nxt)
            else:
                seen = None
        acc_cur[...] = acc

    @pl.when(i % 2 == 0)
    def _():
        step(a0, a1, acc0, acc1)

    @pl.when(i % 2 == 1)
    def _():
        step(a1, a0, acc1, acc0)


def _mlp(h, pre_g, w_up, w_down, post_g, *, layer, w_layer, tm, cast=(), cast_layer=0):
    rows = h.shape[0]
    n_tiles = rows // tm
    depth = pre_g.shape[0]
    const = lambda i: (0, 0)
    of_w_layer = lambda i: (w_layer, 0, 0)
    next_map = lambda i: (jnp.minimum(i + 1, n_tiles - 1), 0)
    prev_map = lambda i: (jnp.maximum(i - 1, 0), 0)
    slab_idx = lambda i: jnp.minimum(i, n_tiles - 1)
    resident = dict(pipeline_mode=pl.Buffered(1))
    slab = lambda w: (None, w.shape[1] // n_tiles, w.shape[2])
    outs = pl.pallas_call(
        functools.partial(_mlp_kernel, layer=layer, tm=tm, n_cast=len(cast)),
        grid=(n_tiles + 1,),
        in_specs=[
            pl.BlockSpec((tm, D_MODEL), next_map),
            pl.BlockSpec((tm, D_MODEL), prev_map),
            pl.BlockSpec((depth, D_MODEL), const),
            pl.BlockSpec((None, D_MODEL, D_FF), of_w_layer, **resident),
            pl.BlockSpec((None, D_FF, D_MODEL), of_w_layer, **resident),
            pl.BlockSpec((depth, D_MODEL), const),
        ] + [pl.BlockSpec(slab(w), lambda i: (cast_layer, slab_idx(i), 0)) for w in cast],
        out_specs=[pl.BlockSpec((tm, D_MODEL), prev_map)]
        + [pl.BlockSpec(slab(w), lambda i: (0, slab_idx(i), 0)) for w in cast],
        out_shape=[jax.ShapeDtypeStruct((rows, D_MODEL), jnp.float32)]
        + [jax.ShapeDtypeStruct((1,) + w.shape[1:], jnp.bfloat16) for w in cast],
        scratch_shapes=[
            pltpu.VMEM((tm, D_MODEL), jnp.bfloat16),
            pltpu.VMEM((tm, D_MODEL), jnp.bfloat16),
            pltpu.VMEM((tm, D_MODEL), jnp.float32),
            pltpu.VMEM((tm, D_MODEL), jnp.float32),
        ],
        compiler_params=pltpu.CompilerParams(
            dimension_semantics=("arbitrary",),
            vmem_limit_bytes=VMEM_LIMIT_BYTES),
        name="mlp",
    )(h, h, pre_g, w_up, w_down, post_g, *cast)
    return outs[0], list(outs[1:])


def _rope_table(first_pos, n_pos, lead_rows=0):
    pos = first_pos + np.arange(n_pos, dtype=np.float64)
    inv_freq = np.power(ROPE_THETA, -np.arange(0, ROT_DIM, 2, dtype=np.float64) / ROT_DIM)
    half = ROT_DIM // 2
    d = np.arange(LANES) % HEAD_DIM
    ang = pos[:, None] * inv_freq[d % half][None, :]
    c = np.where(d < ROT_DIM, np.cos(ang), 1.0)
    s = np.where(d < half, -np.sin(ang), np.where(d < ROT_DIM, np.sin(ang), 0.0))
    tab = np.concatenate([c, s], axis=1).astype(np.float32)
    return jnp.asarray(np.pad(tab, ((lead_rows, 0), (0, 0))))


def kernel(x, meta_tokens, mix_pre_g, w_in, conv_w, sinks, attn_out_g, conv_out_g, w_out,
           mix_post_g, mlp_pre_g, w_up, w_down, mlp_post_g):
    batch, seq, _ = x.shape
    depth = w_in.shape[0]
    bf16 = jnp.bfloat16
    f32 = jnp.float32
    lead = QB - N_META

    tab_meta = _rope_table(0, N_META, lead_rows=lead)
    tab_x = _rope_table(N_META, seq)

    hx = x.reshape(batch * seq, D_MODEL)
    hm = jnp.pad(meta_tokens.astype(x.dtype), ((lead, 0), (0, 0)))

    masked = np.full((WINDOW, 1), MASKED, np.float32)
    tail_valid = np.where(np.arange(WINDOW)[:, None] >= lead, 0.0, MASKED).astype(np.float32)
    cb_meta = jnp.asarray(np.concatenate([masked, tail_valid]))
    cb_x = jnp.asarray(np.concatenate([tail_valid, np.zeros((QB, 1), np.float32)]))
    zero_k = jnp.zeros((N_KV_HEADS, WINDOW, HEAD_DIM), bf16)
    zero_vt = jnp.zeros((KV_WIDTH, WINDOW), f32)
    zero_u = jnp.zeros((CONV_CARRY_ROWS, CONV_WIDTH), f32)

    attn_g = attn_out_g.reshape(depth, -1, 1)
    w_in_b, w_out_b = w_in[0:1].astype(bf16), w_out[0:1].astype(bf16)
    for l in range(depth):
        mixer_w = (sinks, mix_pre_g, w_in_b, conv_w, attn_g, conv_out_g, w_out_b, mix_post_g)
        hm, mk, mv, mu = _mixer(hm, tab_meta, zero_k, zero_vt, zero_u, cb_meta, *mixer_w,
                                layer=l, w_layer=0, batch=1, tm=QB)
        hx, _, _, _, w_up_b, w_down_b = _mixer(
            hx, tab_x, mk[0], mv[0], mu[0], cb_x, *mixer_w, layer=l, w_layer=0, batch=batch,
            tm=ROW_TILE, cast=(w_up, w_down), cast_layer=l)
        mlp_w = (mlp_pre_g, w_up_b, w_down_b, mlp_post_g)
        if l + 1 < depth:
            hm, _ = _mlp(hm, *mlp_w, layer=l, w_layer=0, tm=QB)
            hx, (w_in_b, w_out_b) = _mlp(hx, *mlp_w, layer=l, w_layer=0, tm=MLP_TILE,
                                         cast=(w_in, w_out), cast_layer=l + 1)
        else:
            hx, _ = _mlp(hx, *mlp_w, layer=l, w_layer=0, tm=MLP_TILE)
    return hx.reshape(batch, seq, D_MODEL)
```

```python
import functools

import jax
import jax.numpy as jnp
import numpy as np
from jax import lax
from jax.experimental import pallas as pl
from jax.experimental.pallas import tpu as pltpu

D_MODEL = 1024
N_META = 16
ATTN_WIDTH = 512
CONV_WIDTH = 512
HEAD_DIM = 64
N_Q_HEADS = 8
N_KV_HEADS = 2
GROUP = N_Q_HEADS // N_KV_HEADS
KV_WIDTH = N_KV_HEADS * HEAD_DIM
CONV_K = 3
WINDOW = 128
ROPE_THETA = 500000.0
ROT_DIM = HEAD_DIM // 4
D_FF = 4 * D_MODEL
IN_WIDTH = ATTN_WIDTH + 2 * KV_WIDTH + 3 * CONV_WIDTH
EPS = 1e-6

S_Q = ATTN_WIDTH
S_K = S_Q + KV_WIDTH
S_V = S_K + KV_WIDTH
S_B = S_V + CONV_WIDTH
S_C = S_B + CONV_WIDTH

LANES = 128
QB = WINDOW
KB = WINDOW + QB
CONV_CARRY_ROWS = 8
MASKED = -1e30
LOG2_E = 1.4426950408889634
SCORE_SCALE_LOG2 = HEAD_DIM ** -0.5 * LOG2_E
ROW_TILE = 512
FF_CHUNK = 4096
MLP_TILE = 1024
MLP_SUB_ROWS = 512
W_IN_CHUNK = 256
SCORE_LOOKAHEAD = 1
PV_DELAY = 1
VMEM_LIMIT_BYTES = 56 * 1024 * 1024


def _rms(x):
    return x * lax.rsqrt(jnp.mean(x * x, axis=-1, keepdims=True) + EPS)


def _rope(t, tab, low_half):
    half = ROT_DIM // 2
    partner = jnp.where(low_half,
                        pltpu.roll(t, LANES - half, axis=1),
                        pltpu.roll(t, half, axis=1))
    return t * tab[:, 0:LANES] + partner * tab[:, LANES:2 * LANES]


def _low_half(tm):
    return lax.broadcasted_iota(jnp.int32, (tm, LANES), 1) % HEAD_DIM < ROT_DIM // 2


def _post_q(chunk, proj_buf, tab_ref, q_buf, *, tm):
    tab = tab_ref[...]
    for c in range(chunk * W_IN_CHUNK // LANES, (chunk + 1) * W_IN_CHUNK // LANES):
        qc = _rope(proj_buf[:, c * LANES:(c + 1) * LANES], tab, _low_half(tm)
                   ).astype(jnp.bfloat16)
        q_buf[2 * c] = qc[:, :HEAD_DIM]
        q_buf[2 * c + 1] = qc[:, HEAD_DIM:]


def _post_kv(proj_buf, tab_ref, k_buf, vt_buf, ok_ref, ov_ref, *, tm):
    kr = (_rope(proj_buf[:, S_Q:S_K], tab_ref[...], _low_half(tm)) * SCORE_SCALE_LOG2
          ).astype(jnp.bfloat16)
    for g in range(N_KV_HEADS):
        kg = kr[:, g * HEAD_DIM:(g + 1) * HEAD_DIM]
        k_buf[g, WINDOW:WINDOW + tm, :] = kg
        ok_ref[0, g] = kg[tm - WINDOW:, :]
    vt = proj_buf[:, S_K:S_V].T
    vt_buf[:, WINDOW:WINDOW + tm] = vt
    ov_ref[0] = vt[:, tm - WINDOW:]


def _post_conv(half, proj_buf, conv_w_ref, ou_ref, u_buf, ycf_buf, *, tm):
    cols = slice(half * W_IN_CHUNK, (half + 1) * W_IN_CHUNK)
    gate = lambda start: proj_buf[:, start + half * W_IN_CHUNK:start + (half + 1) * W_IN_CHUNK]
    u = gate(S_B) * gate(S_C)
    u_buf[CONV_CARRY_ROWS:CONV_CARRY_ROWS + tm, cols] = u
    ou_ref[0, :, cols] = u[tm - CONV_CARRY_ROWS:, :]
    cw = conv_w_ref[:, cols]
    y_conv = cw[2:3, :] * u
    y_conv += cw[1:2, :] * u_buf[CONV_CARRY_ROWS - 1:CONV_CARRY_ROWS - 1 + tm, cols]
    y_conv += cw[0:1, :] * u_buf[CONV_CARRY_ROWS - 2:CONV_CARRY_ROWS - 2 + tm, cols]
    ycf_buf[:, cols] = gate(S_V) * y_conv


def _post_conv_norm(ycf_buf, conv_g_ref, yc_buf):
    yc_buf[...] = (_rms(ycf_buf[...]) * conv_g_ref[...]).astype(jnp.bfloat16)


def _scores(item, first_tile, q_buf, k_buf, bias_buf):
    j, g = divmod(item, N_KV_HEADS)
    r0 = j * QB
    bias = bias_buf[jnp.where(first_tile, 0, 1)] if j == 0 else bias_buf[1]
    kg = k_buf[g, r0:r0 + KB, :]
    qg = q_buf[g * GROUP:(g + 1) * GROUP, r0:r0 + QB, :].reshape(GROUP * QB, HEAD_DIM)
    return lax.dot_general(kg, qg, (((1,), (1,)), ((), ())),
                           preferred_element_type=jnp.float32) + bias


def _softmax(item, s, layer, sinks_ref):
    g = item % N_KV_HEADS
    sink = jnp.concatenate(
        [jnp.full((1, QB), sinks_ref[layer, g * GROUP + i] * LOG2_E, jnp.float32)
         for i in range(GROUP)], axis=1)
    m = jnp.maximum(jnp.max(s, axis=0, keepdims=True), sink)
    e = jnp.exp2(s - m)
    den = jnp.sum(e, axis=0, keepdims=True) + jnp.exp2(sink - m)
    return e.astype(jnp.bfloat16), 1.0 / den


def _pv(item, e, inv_den, vt_buf, yat_buf):
    j, g = divmod(item, N_KV_HEADS)
    r0 = j * QB
    vt = vt_buf[g * HEAD_DIM:(g + 1) * HEAD_DIM, r0:r0 + KB].astype(jnp.bfloat16)
    o = jnp.dot(vt, e, preferred_element_type=jnp.float32) * inv_den
    for i in range(GROUP):
        hd = g * GROUP + i
        yat_buf[hd * HEAD_DIM:(hd + 1) * HEAD_DIM, r0:r0 + QB] = o[:, i * QB:(i + 1) * QB]


def _mixer_out_post(h_ref, attn_g_ref, w_out_ref, post_g_ref, o_ref, yc_buf, yat_buf):
    yat = yat_buf[...]
    inv = lax.rsqrt(jnp.mean(yat * yat, axis=0, keepdims=True) + EPS)
    ya_n = (yat * inv * attn_g_ref[...]).astype(jnp.bfloat16)
    mixed = lax.dot_general(ya_n, w_out_ref[0:ATTN_WIDTH, :], (((0,), (0,)), ((), ())),
                            preferred_element_type=jnp.float32)
    mixed += jnp.dot(yc_buf[...], w_out_ref[ATTN_WIDTH:, :], preferred_element_type=jnp.float32)
    o_ref[...] = h_ref[...] + _rms(mixed) * post_g_ref[...]


N_MIXER_INPUTS = 15
N_MIXER_OUTPUTS = 4


def _mixer_kernel(*refs, layer, tm, n_tiles, n_steps, n_cast):
    (sinks_ref, h_in_ref, h_out_ref, tab_ref, ck_ref, cv_ref, cu_ref, cb_ref,
     pre_g_ref, w_in_ref, conv_w_ref, attn_g_ref, conv_g_ref, w_out_ref, post_g_ref
     ) = refs[:N_MIXER_INPUTS]
    cast_src = refs[N_MIXER_INPUTS:N_MIXER_INPUTS + n_cast]
    outs = refs[N_MIXER_INPUTS + n_cast:]
    o_ref, ok_ref, ov_ref, ou_ref = outs[:N_MIXER_OUTPUTS]
    cast_dst = outs[N_MIXER_OUTPUTS:N_MIXER_OUTPUTS + n_cast]
    (q0, k0, vt0, u0, yc0, q1, k1, vt1, u1, yc1, yat_buf, bias_buf, a_buf, proj_buf, ycf_buf
     ) = outs[N_MIXER_OUTPUTS + n_cast:]
    pre_g_ref, conv_g_ref, post_g_ref = (
        g.at[layer:layer + 1] for g in (pre_g_ref, conv_g_ref, post_g_ref))
    for src, dst in zip(cast_src, cast_dst):
        dst[...] = src[...].astype(jnp.bfloat16)
    s = pl.program_id(0)
    tile_in = jnp.minimum(s, n_steps - 1) % n_tiles
    tile_out = jnp.maximum(s - 1, 0) % n_tiles
    slots = ((q0, k0, vt0, u0, yc0), (q1, k1, vt1, u1, yc1))

    @pl.when(s == 0)
    def _():
        key_idx = lax.broadcasted_iota(jnp.int32, (KB, QB), 0)
        qry_idx = lax.broadcasted_iota(jnp.int32, (KB, QB), 1)
        band = jnp.where((key_idx > qry_idx) & (key_idx <= qry_idx + WINDOW), 0.0, MASKED)
        band = jnp.concatenate([band] * GROUP, axis=1)
        bias_buf[1] = band
        bias_buf[0] = band + cb_ref[...]
        q1[...] = jnp.zeros(q1.shape, q1.dtype)
        k1[...] = jnp.zeros(k1.shape, k1.dtype)
        vt1[...] = jnp.zeros(vt1.shape, vt1.dtype)
        yc1[...] = jnp.zeros(yc1.shape, yc1.dtype)

    def step(cur, prev):
        q_c, k_c, vt_c, u_c, yc_c = cur
        q_p, k_p, vt_p, u_p, yc_p = prev

        @pl.when(tile_in == 0)
        def _():
            k_c[:, 0:WINDOW, :] = ck_ref[...]
            vt_c[:, 0:WINDOW] = cv_ref[...]
            u_c[0:CONV_CARRY_ROWS, :] = cu_ref[...]

        @pl.when(tile_in != 0)
        def _():
            k_c[:, 0:WINDOW, :] = k_p[:, tm:tm + WINDOW, :]
            vt_c[:, 0:WINDOW] = vt_p[:, tm:tm + WINDOW]
            u_c[0:CONV_CARRY_ROWS, :] = u_p[tm:tm + CONV_CARRY_ROWS, :]

        a_buf[...] = (_rms(h_in_ref[...]) * pre_g_ref[...]).astype(jnp.bfloat16)

        def w_in_chunk(c):
            cols = slice(c * W_IN_CHUNK, (c + 1) * W_IN_CHUNK)
            proj_buf[:, cols] = jnp.dot(a_buf[...], w_in_ref[:, cols],
                                        preferred_element_type=jnp.float32)

        q_of = functools.partial(_post_q, proj_buf=proj_buf, tab_ref=tab_ref, q_buf=q_c, tm=tm)
        conv_of = functools.partial(_post_conv, proj_buf=proj_buf, conv_w_ref=conv_w_ref,
                                    ou_ref=ou_ref, u_buf=u_c, ycf_buf=ycf_buf, tm=tm)

        def conv_tail():
            conv_of(1)
            _post_conv_norm(ycf_buf, conv_g_ref, yc_c)

        chunk_plan = [
            (0, lambda: q_of(0)), (1, lambda: q_of(1)),
            (5, None), (7, None), (3, lambda: conv_of(0)),
            (6, None), (8, None), (4, conv_tail),
            (2, lambda: _post_kv(proj_buf, tab_ref, k_c, vt_c, ok_ref, ov_ref, tm=tm)),
        ]
        assert sorted(c for c, _ in chunk_plan) == list(range(IN_WIDTH // W_IN_CHUNK))

        def run_chunk(entry):
            c, post = entry
            w_in_chunk(c)
            if post is not None:
                post()

        n_items = (tm // QB) * N_KV_HEADS
        first_out = tile_out == 0
        pending = [_scores(i, first_out, q_p, k_p, bias_buf)
                   for i in range(min(SCORE_LOOKAHEAD, n_items))]
        late = []
        for item in range(n_items):
            s_cur = pending.pop(0)
            if item + SCORE_LOOKAHEAD < n_items:
                pending.append(_scores(item + SCORE_LOOKAHEAD, first_out, q_p, k_p, bias_buf))
            if len(late) == PV_DELAY:
                _pv(*late.pop(0), vt_p, yat_buf)
            if item < len(chunk_plan):
                run_chunk(chunk_plan[item])
            late.append((item,) + _softmax(item, s_cur, layer, sinks_ref))
        for entry in late:
            _pv(*entry, vt_p, yat_buf)
        for entry in chunk_plan[n_items:]:
            run_chunk(entry)
        _mixer_out_post(h_out_ref, attn_g_ref, w_out_ref, post_g_ref, o_ref, yc_p, yat_buf)

    @pl.when(s % 2 == 0)
    def _():
        step(slots[0], slots[1])

    @pl.when(s % 2 == 1)
    def _():
        step(slots[1], slots[0])


def _mixer(h, tab, ck, cv, cu, cb, sinks, pre_g, w_in, conv_w, attn_g, conv_g, w_out, post_g,
           *, layer, w_layer, batch, tm, cast=(), cast_layer=0):
    rows = h.shape[0]
    n_steps = rows // tm
    n_tiles = n_steps // batch
    const = lambda s: (0, 0)
    const3 = lambda s: (0, 0, 0)
    of_layer = lambda s: (layer, 0, 0)
    of_w_layer = lambda s: (w_layer, 0, 0)
    in_map = lambda s: (jnp.minimum(s, n_steps - 1), 0)
    out_map = lambda s: (jnp.maximum(s - 1, 0), 0)
    seq_of_in = lambda s: jnp.minimum(s, n_steps - 1) // n_tiles
    resident = dict(pipeline_mode=pl.Buffered(1))
    depth = pre_g.shape[0]
    slab = lambda w: (None, w.shape[1] // n_steps, w.shape[2])
    kernel = functools.partial(_mixer_kernel, layer=layer, tm=tm, n_tiles=n_tiles,
                               n_steps=n_steps, n_cast=len(cast))
    slot = [
        pltpu.VMEM((N_Q_HEADS, tm, HEAD_DIM), jnp.bfloat16),
        pltpu.VMEM((N_KV_HEADS, WINDOW + tm, HEAD_DIM), jnp.bfloat16),
        pltpu.VMEM((KV_WIDTH, WINDOW + tm), jnp.float32),
        pltpu.VMEM((CONV_CARRY_ROWS + tm, CONV_WIDTH), jnp.float32),
        pltpu.VMEM((tm, CONV_WIDTH), jnp.bfloat16),
    ]
    return pl.pallas_call(
        kernel,
        grid=(n_steps + 1,),
        in_specs=[
            pl.BlockSpec(memory_space=pltpu.SMEM),
            pl.BlockSpec((tm, D_MODEL), in_map),
            pl.BlockSpec((tm, D_MODEL), out_map),
            pl.BlockSpec((tm, 2 * LANES),
                         lambda s: (jnp.minimum(s, n_steps - 1) % n_tiles, 0)),
            pl.BlockSpec((N_KV_HEADS, WINDOW, HEAD_DIM), const3),
            pl.BlockSpec((KV_WIDTH, WINDOW), const),
            pl.BlockSpec((CONV_CARRY_ROWS, CONV_WIDTH), const),
            pl.BlockSpec((KB, 1), const),
            pl.BlockSpec((depth, D_MODEL), const),
            pl.BlockSpec((None, D_MODEL, IN_WIDTH), of_w_layer, **resident),
            pl.BlockSpec((None, CONV_K, CONV_WIDTH), of_layer),
            pl.BlockSpec((None, ATTN_WIDTH, 1), of_layer),
            pl.BlockSpec((depth, CONV_WIDTH), const),
            pl.BlockSpec((None, D_MODEL, D_MODEL), of_w_layer, **resident),
            pl.BlockSpec((depth, D_MODEL), const),
        ] + [pl.BlockSpec(slab(w), lambda s: (cast_layer, jnp.minimum(s, n_steps - 1), 0))
             for w in cast],
        out_specs=[
            pl.BlockSpec((tm, D_MODEL), out_map),
            pl.BlockSpec((1, N_KV_HEADS, WINDOW, HEAD_DIM), lambda s: (seq_of_in(s), 0, 0, 0)),
            pl.BlockSpec((1, KV_WIDTH, WINDOW), lambda s: (seq_of_in(s), 0, 0)),
            pl.BlockSpec((1, CONV_CARRY_ROWS, CONV_WIDTH), lambda s: (seq_of_in(s), 0, 0)),
        ] + [pl.BlockSpec(slab(w), lambda s: (0, jnp.minimum(s, n_steps - 1), 0)) for w in cast],
        out_shape=[
            jax.ShapeDtypeStruct((rows, D_MODEL), jnp.float32),
            jax.ShapeDtypeStruct((batch, N_KV_HEADS, WINDOW, HEAD_DIM), jnp.bfloat16),
            jax.ShapeDtypeStruct((batch, KV_WIDTH, WINDOW), jnp.float32),
            jax.ShapeDtypeStruct((batch, CONV_CARRY_ROWS, CONV_WIDTH), jnp.float32),
        ] + [jax.ShapeDtypeStruct((1,) + w.shape[1:], jnp.bfloat16) for w in cast],
        scratch_shapes=slot + slot + [
            pltpu.VMEM((ATTN_WIDTH, tm), jnp.float32),
            pltpu.VMEM((2, KB, GROUP * QB), jnp.float32),
            pltpu.VMEM((tm, D_MODEL), jnp.bfloat16),
            pltpu.VMEM((tm, IN_WIDTH), jnp.float32),
            pltpu.VMEM((tm, CONV_WIDTH), jnp.float32),
        ],
        compiler_params=pltpu.CompilerParams(
            dimension_semantics=("arbitrary",),
            vmem_limit_bytes=VMEM_LIMIT_BYTES),
        name="mixer",
    )(sinks, h, h, tab, ck, cv, cu, cb, pre_g, w_in, conv_w, attn_g, conv_g, w_out, post_g, *cast)


def _mlp_kernel(*refs, layer, tm, n_cast):
    h_ref, pre_g_ref, w_up_ref, w_down_ref, post_g_ref = refs[:5]
    pre_g_ref, post_g_ref = (g.at[layer:layer + 1] for g in (pre_g_ref, post_g_ref))
    cast_src = refs[5:5 + n_cast]
    o_ref = refs[5 + n_cast]
    cast_dst = refs[6 + n_cast:]
    for src, dst in zip(cast_src, cast_dst):
        dst[...] = src[...].astype(jnp.bfloat16)

    sub = min(MLP_SUB_ROWS, tm)
    subs = [slice(r, r + sub) for r in range(0, tm, sub)]
    a = [(_rms(h_ref[rows, :]) * pre_g_ref[...]).astype(jnp.bfloat16) for rows in subs]
    for a_sub, rows in zip(a, subs):
        acc = jnp.zeros((sub, D_MODEL), jnp.float32)
        for c in range(D_FF // FF_CHUNK):
            cols = slice(c * FF_CHUNK, (c + 1) * FF_CHUNK)
            up = jnp.dot(a_sub, w_up_ref[:, cols], preferred_element_type=jnp.float32)
            act = jnp.square(jnp.maximum(up, 0.0)).astype(jnp.bfloat16)
            acc = acc + jnp.dot(act, w_down_ref[cols, :], preferred_element_type=jnp.float32)
        o_ref[rows, :] = h_ref[rows, :] + _rms(acc) * post_g_ref[...]


def _mlp(h, pre_g, w_up, w_down, post_g, *, layer, w_layer, tm, cast=(), cast_layer=0):
    rows = h.shape[0]
    steps = rows // tm
    depth = pre_g.shape[0]
    const = lambda i: (0, 0)
    of_w_layer = lambda i: (w_layer, 0, 0)
    resident = dict(pipeline_mode=pl.Buffered(1))
    slab = lambda w: (None, w.shape[1] // steps, w.shape[2])
    outs = pl.pallas_call(
        functools.partial(_mlp_kernel, layer=layer, tm=tm, n_cast=len(cast)),
        grid=(steps,),
        in_specs=[
            pl.BlockSpec((tm, D_MODEL), lambda i: (i, 0)),
            pl.BlockSpec((depth, D_MODEL), const),
            pl.BlockSpec((None, D_MODEL, D_FF), of_w_layer, **resident),
            pl.BlockSpec((None, D_FF, D_MODEL), of_w_layer, **resident),
            pl.BlockSpec((depth, D_MODEL), const),
        ] + [pl.BlockSpec(slab(w), lambda i: (cast_layer, i, 0)) for w in cast],
        out_specs=[pl.BlockSpec((tm, D_MODEL), lambda i: (i, 0))]
        + [pl.BlockSpec(slab(w), lambda i: (0, i, 0)) for w in cast],
        out_shape=[jax.ShapeDtypeStruct((rows, D_MODEL), jnp.float32)]
        + [jax.ShapeDtypeStruct((1,) + w.shape[1:], jnp.bfloat16) for w in cast],
        compiler_params=pltpu.CompilerParams(
            dimension_semantics=("arbitrary",),
            vmem_limit_bytes=VMEM_LIMIT_BYTES),
        name="mlp",
    )(h, pre_g, w_up, w_down, post_g, *cast)
    return outs[0], list(outs[1:])


def _rope_table(first_pos, n_pos, lead_rows=0):
    pos = first_pos + np.arange(n_pos, dtype=np.float64)
    inv_freq = np.power(ROPE_THETA, -np.arange(0, ROT_DIM, 2, dtype=np.float64) / ROT_DIM)
    half = ROT_DIM // 2
    d = np.arange(LANES) % HEAD_DIM
    ang = pos[:, None] * inv_freq[d % half][None, :]
    c = np.where(d < ROT_DIM, np.cos(ang), 1.0)
    s = np.where(d < half, -np.sin(ang), np.where(d < ROT_DIM, np.sin(ang), 0.0))
    tab = np.concatenate([c, s], axis=1).astype(np.float32)
    return jnp.asarray(np.pad(tab, ((lead_rows, 0), (0, 0))))


def kernel(x, meta_tokens, mix_pre_g, w_in, conv_w, sinks, attn_out_g, conv_out_g, w_out,
           mix_post_g, mlp_pre_g, w_up, w_down, mlp_post_g):
    batch, seq, _ = x.shape
    depth = w_in.shape[0]
    bf16 = jnp.bfloat16
    f32 = jnp.float32
    lead = QB - N_META

    tab_meta = _rope_table(0, N_META, lead_rows=lead)
    tab_x = _rope_table(N_META, seq)

    hx = x.reshape(batch * seq, D_MODEL)
    hm = jnp.pad(meta_tokens.astype(x.dtype), ((lead, 0), (0, 0)))

    masked = np.full((WINDOW, 1), MASKED, np.float32)
    tail_valid = np.where(np.arange(WINDOW)[:, None] >= lead, 0.0, MASKED).astype(np.float32)
    cb_meta = jnp.asarray(np.concatenate([masked, tail_valid]))
    cb_x = jnp.asarray(np.concatenate([tail_valid, np.zeros((QB, 1), np.float32)]))
    zero_k = jnp.zeros((N_KV_HEADS, WINDOW, HEAD_DIM), bf16)
    zero_vt = jnp.zeros((KV_WIDTH, WINDOW), f32)
    zero_u = jnp.zeros((CONV_CARRY_ROWS, CONV_WIDTH), f32)

    attn_g = attn_out_g.reshape(depth, -1, 1)
    w_in_b, w_out_b = w_in[0:1].astype(bf16), w_out[0:1].astype(bf16)
    for l in range(depth):
        mixer_w = (sinks, mix_pre_g, w_in_b, conv_w, attn_g, conv_out_g, w_out_b, mix_post_g)
        hm, mk, mv, mu = _mixer(hm, tab_meta, zero_k, zero_vt, zero_u, cb_meta, *mixer_w,
                                layer=l, w_layer=0, batch=1, tm=QB)
        hx, _, _, _, w_up_b, w_down_b = _mixer(
            hx, tab_x, mk[0], mv[0], mu[0], cb_x, *mixer_w, layer=l, w_layer=0, batch=batch,
            tm=ROW_TILE, cast=(w_up, w_down), cast_layer=l)
        mlp_w = (mlp_pre_g, w_up_b, w_down_b, mlp_post_g)
        if l + 1 < depth:
            hm, _ = _mlp(hm, *mlp_w, layer=l, w_layer=0, tm=QB)
            hx, (w_in_b, w_out_b) = _mlp(hx, *mlp_w, layer=l, w_layer=0, tm=MLP_TILE,
                                         cast=(w_in, w_out), cast_layer=l + 1)
        else:
            hx, _ = _mlp(hx, *mlp_w, layer=l, w_layer=0, tm=MLP_TILE)
    return hx.reshape(batch, seq, D_MODEL)
```

```python
import functools

import jax
import jax.numpy as jnp
import numpy as np
from jax import lax
from jax.experimental import pallas as pl
from jax.experimental.pallas import tpu as pltpu

D_MODEL = 1024
N_META = 16
ATTN_WIDTH = 512
CONV_WIDTH = 512
HEAD_DIM = 64
N_Q_HEADS = 8
N_KV_HEADS = 2
GROUP = N_Q_HEADS // N_KV_HEADS
KV_WIDTH = N_KV_HEADS * HEAD_DIM
CONV_K = 3
WINDOW = 128
ROPE_THETA = 500000.0
ROT_DIM = HEAD_DIM // 4
D_FF = 4 * D_MODEL
IN_WIDTH = ATTN_WIDTH + 2 * KV_WIDTH + 3 * CONV_WIDTH
EPS = 1e-6

S_Q = ATTN_WIDTH
S_K = S_Q + KV_WIDTH
S_V = S_K + KV_WIDTH
S_B = S_V + CONV_WIDTH
S_C = S_B + CONV_WIDTH

LANES = 128
QB = WINDOW
KB = WINDOW + QB
CONV_CARRY_ROWS = 8
MASKED = -1e30
LOG2_E = 1.4426950408889634
SCORE_SCALE_LOG2 = HEAD_DIM ** -0.5 * LOG2_E
ROW_TILE = 512
FF_CHUNK = 4096
MLP_TILE = 1024
MLP_SUB_ROWS = 512
W_IN_CHUNK = 256
SCORE_LOOKAHEAD = 1
PV_DELAY = 1
V7X_VMEM_BYTES = 64 * 1024 * 1024
MIXER_VMEM_BYTES = V7X_VMEM_BYTES * 7 // 8
MLP_VMEM_BYTES = V7X_VMEM_BYTES * 3 // 4
META_VMEM_BYTES = V7X_VMEM_BYTES // 2


def _rms(x):
    return x * lax.rsqrt(jnp.mean(x * x, axis=-1, keepdims=True) + EPS)


def _rope(t, tab, low_half):
    half = ROT_DIM // 2
    partner = jnp.where(low_half,
                        pltpu.roll(t, LANES - half, axis=1),
                        pltpu.roll(t, half, axis=1))
    return t * tab[:, 0:LANES] + partner * tab[:, LANES:2 * LANES]


def _low_half(tm):
    return lax.broadcasted_iota(jnp.int32, (tm, LANES), 1) % HEAD_DIM < ROT_DIM // 2


def _post_q(chunk, proj_buf, tab_ref, q_buf, *, tm):
    tab = tab_ref[...]
    for c in range(chunk * W_IN_CHUNK // LANES, (chunk + 1) * W_IN_CHUNK // LANES):
        qc = _rope(proj_buf[:, c * LANES:(c + 1) * LANES], tab, _low_half(tm)
                   ).astype(jnp.bfloat16)
        q_buf[2 * c] = qc[:, :HEAD_DIM]
        q_buf[2 * c + 1] = qc[:, HEAD_DIM:]


def _post_kv(proj_buf, tab_ref, k_buf, vt_buf, ok_ref, ov_ref, *, tm):
    kr = (_rope(proj_buf[:, S_Q:S_K], tab_ref[...], _low_half(tm)) * SCORE_SCALE_LOG2
          ).astype(jnp.bfloat16)
    for g in range(N_KV_HEADS):
        kg = kr[:, g * HEAD_DIM:(g + 1) * HEAD_DIM]
        k_buf[g, WINDOW:WINDOW + tm, :] = kg
        ok_ref[0, g] = kg[tm - WINDOW:, :]
    vt = proj_buf[:, S_K:S_V].T
    vt_buf[:, WINDOW:WINDOW + tm] = vt
    ov_ref[0] = vt[:, tm - WINDOW:]


def _post_conv(half, proj_buf, conv_w_ref, ou_ref, u_buf, ycf_buf, *, tm):
    cols = slice(half * W_IN_CHUNK, (half + 1) * W_IN_CHUNK)
    gate = lambda start: proj_buf[:, start + half * W_IN_CHUNK:start + (half + 1) * W_IN_CHUNK]
    u = gate(S_B) * gate(S_C)
    u_buf[CONV_CARRY_ROWS:CONV_CARRY_ROWS + tm, cols] = u
    ou_ref[0, :, cols] = u[tm - CONV_CARRY_ROWS:, :]
    cw = conv_w_ref[:, cols]
    y_conv = cw[2:3, :] * u
    y_conv += cw[1:2, :] * u_buf[CONV_CARRY_ROWS - 1:CONV_CARRY_ROWS - 1 + tm, cols]
    y_conv += cw[0:1, :] * u_buf[CONV_CARRY_ROWS - 2:CONV_CARRY_ROWS - 2 + tm, cols]
    ycf_buf[:, cols] = gate(S_V) * y_conv


def _post_conv_norm(ycf_buf, conv_g_ref, yc_buf):
    yc_buf[...] = (_rms(ycf_buf[...]) * conv_g_ref[...]).astype(jnp.bfloat16)


def _scores(item, first_tile, q_buf, k_buf, bias_buf):
    j, g = divmod(item, N_KV_HEADS)
    r0 = j * QB
    bias = bias_buf[jnp.where(first_tile, 0, 1)] if j == 0 else bias_buf[1]
    kg = k_buf[g, r0:r0 + KB, :]
    qg = q_buf[g * GROUP:(g + 1) * GROUP, r0:r0 + QB, :].reshape(GROUP * QB, HEAD_DIM)
    return lax.dot_general(kg, qg, (((1,), (1,)), ((), ())),
                           preferred_element_type=jnp.float32) + bias


def _softmax(item, s, layer, sinks_ref):
    g = item % N_KV_HEADS
    sink = jnp.concatenate(
        [jnp.full((1, QB), sinks_ref[layer, g * GROUP + i] * LOG2_E, jnp.float32)
         for i in range(GROUP)], axis=1)
    m = jnp.maximum(jnp.max(s, axis=0, keepdims=True), sink)
    e = jnp.exp2(s - m)
    den = jnp.sum(e, axis=0, keepdims=True) + jnp.exp2(sink - m)
    return e.astype(jnp.bfloat16), 1.0 / den


def _pv(item, e, inv_den, vt_buf, yat_buf):
    j, g = divmod(item, N_KV_HEADS)
    r0 = j * QB
    vt = vt_buf[g * HEAD_DIM:(g + 1) * HEAD_DIM, r0:r0 + KB].astype(jnp.bfloat16)
    o = jnp.dot(vt, e, preferred_element_type=jnp.float32) * inv_den
    for i in range(GROUP):
        hd = g * GROUP + i
        yat_buf[hd * HEAD_DIM:(hd + 1) * HEAD_DIM, r0:r0 + QB] = o[:, i * QB:(i + 1) * QB]


def _mixer_out_post(h_ref, attn_g_ref, w_out_ref, post_g_ref, o_ref, yc_buf, yat_buf):
    yat = yat_buf[...]
    inv = lax.rsqrt(jnp.mean(yat * yat, axis=0, keepdims=True) + EPS)
    ya_n = (yat * inv * attn_g_ref[...]).astype(jnp.bfloat16)
    mixed = lax.dot_general(ya_n, w_out_ref[0:ATTN_WIDTH, :], (((0,), (0,)), ((), ())),
                            preferred_element_type=jnp.float32)
    mixed += jnp.dot(yc_buf[...], w_out_ref[ATTN_WIDTH:, :], preferred_element_type=jnp.float32)
    o_ref[...] = h_ref[...] + _rms(mixed) * post_g_ref[...]


N_MIXER_INPUTS = 15
N_MIXER_OUTPUTS = 4


def _mixer_kernel(*refs, layer, tm, n_tiles, n_steps, n_cast):
    (sinks_ref, h_in_ref, h_out_ref, tab_ref, ck_ref, cv_ref, cu_ref, cb_ref,
     pre_g_ref, w_in_ref, conv_w_ref, attn_g_ref, conv_g_ref, w_out_ref, post_g_ref
     ) = refs[:N_MIXER_INPUTS]
    cast_src = refs[N_MIXER_INPUTS:N_MIXER_INPUTS + n_cast]
    outs = refs[N_MIXER_INPUTS + n_cast:]
    o_ref, ok_ref, ov_ref, ou_ref = outs[:N_MIXER_OUTPUTS]
    cast_dst = outs[N_MIXER_OUTPUTS:N_MIXER_OUTPUTS + n_cast]
    (q0, k0, vt0, u0, yc0, q1, k1, vt1, u1, yc1, yat_buf, bias_buf, a_buf, proj_buf, ycf_buf
     ) = outs[N_MIXER_OUTPUTS + n_cast:]
    pre_g_ref, conv_g_ref, post_g_ref = (
        g.at[layer:layer + 1] for g in (pre_g_ref, conv_g_ref, post_g_ref))
    for src, dst in zip(cast_src, cast_dst):
        dst[...] = src[...].astype(jnp.bfloat16)
    s = pl.program_id(0)
    tile_in = jnp.minimum(s, n_steps - 1) % n_tiles
    tile_out = jnp.maximum(s - 1, 0) % n_tiles
    slots = ((q0, k0, vt0, u0, yc0), (q1, k1, vt1, u1, yc1))

    @pl.when(s == 0)
    def _():
        key_idx = lax.broadcasted_iota(jnp.int32, (KB, QB), 0)
        qry_idx = lax.broadcasted_iota(jnp.int32, (KB, QB), 1)
        band = jnp.where((key_idx > qry_idx) & (key_idx <= qry_idx + WINDOW), 0.0, MASKED)
        band = jnp.concatenate([band] * GROUP, axis=1)
        bias_buf[1] = band
        bias_buf[0] = band + cb_ref[...]
        q1[...] = jnp.zeros(q1.shape, q1.dtype)
        k1[...] = jnp.zeros(k1.shape, k1.dtype)
        vt1[...] = jnp.zeros(vt1.shape, vt1.dtype)
        yc1[...] = jnp.zeros(yc1.shape, yc1.dtype)

    def step(cur, prev):
        q_c, k_c, vt_c, u_c, yc_c = cur
        q_p, k_p, vt_p, u_p, yc_p = prev

        @pl.when(tile_in == 0)
        def _():
            k_c[:, 0:WINDOW, :] = ck_ref[...]
            vt_c[:, 0:WINDOW] = cv_ref[...]
            u_c[0:CONV_CARRY_ROWS, :] = cu_ref[...]

        @pl.when(tile_in != 0)
        def _():
            k_c[:, 0:WINDOW, :] = k_p[:, tm:tm + WINDOW, :]
            vt_c[:, 0:WINDOW] = vt_p[:, tm:tm + WINDOW]
            u_c[0:CONV_CARRY_ROWS, :] = u_p[tm:tm + CONV_CARRY_ROWS, :]

        a_buf[...] = (_rms(h_in_ref[...]) * pre_g_ref[...]).astype(jnp.bfloat16)

        def w_in_chunk(c):
            cols = slice(c * W_IN_CHUNK, (c + 1) * W_IN_CHUNK)
            proj_buf[:, cols] = jnp.dot(a_buf[...], w_in_ref[:, cols],
                                        preferred_element_type=jnp.float32)

        q_of = functools.partial(_post_q, proj_buf=proj_buf, tab_ref=tab_ref, q_buf=q_c, tm=tm)
        conv_of = functools.partial(_post_conv, proj_buf=proj_buf, conv_w_ref=conv_w_ref,
                                    ou_ref=ou_ref, u_buf=u_c, ycf_buf=ycf_buf, tm=tm)

        def conv_tail():
            conv_of(1)
            _post_conv_norm(ycf_buf, conv_g_ref, yc_c)

        chunk_plan = [
            (0, lambda: q_of(0)), (1, lambda: q_of(1)),
            (5, None), (7, None), (3, lambda: conv_of(0)),
            (6, None), (8, None), (4, conv_tail),
            (2, lambda: _post_kv(proj_buf, tab_ref, k_c, vt_c, ok_ref, ov_ref, tm=tm)),
        ]
        assert sorted(c for c, _ in chunk_plan) == list(range(IN_WIDTH // W_IN_CHUNK))

        def run_chunk(entry):
            c, post = entry
            w_in_chunk(c)
            if post is not None:
                post()

        n_items = (tm // QB) * N_KV_HEADS
        first_out = tile_out == 0
        pending = [_scores(i, first_out, q_p, k_p, bias_buf)
                   for i in range(min(SCORE_LOOKAHEAD, n_items))]
        late = []
        for item in range(n_items):
            s_cur = pending.pop(0)
            if item + SCORE_LOOKAHEAD < n_items:
                pending.append(_scores(item + SCORE_LOOKAHEAD, first_out, q_p, k_p, bias_buf))
            if len(late) == PV_DELAY:
                _pv(*late.pop(0), vt_p, yat_buf)
            if item < len(chunk_plan):
                run_chunk(chunk_plan[item])
            late.append((item,) + _softmax(item, s_cur, layer, sinks_ref))
        for entry in late:
            _pv(*entry, vt_p, yat_buf)
        for entry in chunk_plan[n_items:]:
            run_chunk(entry)
        _mixer_out_post(h_out_ref, attn_g_ref, w_out_ref, post_g_ref, o_ref, yc_p, yat_buf)

    @pl.when(s % 2 == 0)
    def _():
        step(slots[0], slots[1])

    @pl.when(s % 2 == 1)
    def _():
        step(slots[1], slots[0])


def _mixer(h, tab, ck, cv, cu, cb, sinks, pre_g, w_in, conv_w, attn_g, conv_g, w_out, post_g,
           *, layer, w_layer, batch, tm, cast=(), cast_layer=0):
    rows = h.shape[0]
    n_steps = rows // tm
    n_tiles = n_steps // batch
    assert tm % QB == 0 and rows == batch * n_tiles * tm, (rows, batch, tm)
    assert all(w.shape[1] % n_steps == 0 for w in cast), [w.shape for w in cast]
    const = lambda s: (0, 0)
    const3 = lambda s: (0, 0, 0)
    of_layer = lambda s: (layer, 0, 0)
    of_w_layer = lambda s: (w_layer, 0, 0)
    in_map = lambda s: (jnp.minimum(s, n_steps - 1), 0)
    out_map = lambda s: (jnp.maximum(s - 1, 0), 0)
    seq_of_in = lambda s: jnp.minimum(s, n_steps - 1) // n_tiles
    resident = dict(pipeline_mode=pl.Buffered(1))
    depth = pre_g.shape[0]
    slab = lambda w: (None, w.shape[1] // n_steps, w.shape[2])
    kernel = functools.partial(_mixer_kernel, layer=layer, tm=tm, n_tiles=n_tiles,
                               n_steps=n_steps, n_cast=len(cast))
    slot = [
        pltpu.VMEM((N_Q_HEADS, tm, HEAD_DIM), jnp.bfloat16),
        pltpu.VMEM((N_KV_HEADS, WINDOW + tm, HEAD_DIM), jnp.bfloat16),
        pltpu.VMEM((KV_WIDTH, WINDOW + tm), jnp.float32),
        pltpu.VMEM((CONV_CARRY_ROWS + tm, CONV_WIDTH), jnp.float32),
        pltpu.VMEM((tm, CONV_WIDTH), jnp.bfloat16),
    ]
    return pl.pallas_call(
        kernel,
        grid=(n_steps + 1,),
        in_specs=[
            pl.BlockSpec(memory_space=pltpu.SMEM),
            pl.BlockSpec((tm, D_MODEL), in_map),
            pl.BlockSpec((tm, D_MODEL), out_map),
            pl.BlockSpec((tm, 2 * LANES),
                         lambda s: (jnp.minimum(s, n_steps - 1) % n_tiles, 0)),
            pl.BlockSpec((N_KV_HEADS, WINDOW, HEAD_DIM), const3),
            pl.BlockSpec((KV_WIDTH, WINDOW), const),
            pl.BlockSpec((CONV_CARRY_ROWS, CONV_WIDTH), const),
            pl.BlockSpec((KB, 1), const),
            pl.BlockSpec((depth, D_MODEL), const),
            pl.BlockSpec((None, D_MODEL, IN_WIDTH), of_w_layer, **resident),
            pl.BlockSpec((None, CONV_K, CONV_WIDTH), of_layer),
            pl.BlockSpec((None, ATTN_WIDTH, 1), of_layer),
            pl.BlockSpec((depth, CONV_WIDTH), const),
            pl.BlockSpec((None, D_MODEL, D_MODEL), of_w_layer, **resident),
            pl.BlockSpec((depth, D_MODEL), const),
        ] + [pl.BlockSpec(slab(w), lambda s: (cast_layer, jnp.minimum(s, n_steps - 1), 0))
             for w in cast],
        out_specs=[
            pl.BlockSpec((tm, D_MODEL), out_map),
            pl.BlockSpec((1, N_KV_HEADS, WINDOW, HEAD_DIM), lambda s: (seq_of_in(s), 0, 0, 0)),
            pl.BlockSpec((1, KV_WIDTH, WINDOW), lambda s: (seq_of_in(s), 0, 0)),
            pl.BlockSpec((1, CONV_CARRY_ROWS, CONV_WIDTH), lambda s: (seq_of_in(s), 0, 0)),
        ] + [pl.BlockSpec(slab(w), lambda s: (0, jnp.minimum(s, n_steps - 1), 0)) for w in cast],
        out_shape=[
            jax.ShapeDtypeStruct((rows, D_MODEL), jnp.float32),
            jax.ShapeDtypeStruct((batch, N_KV_HEADS, WINDOW, HEAD_DIM), jnp.bfloat16),
            jax.ShapeDtypeStruct((batch, KV_WIDTH, WINDOW), jnp.float32),
            jax.ShapeDtypeStruct((batch, CONV_CARRY_ROWS, CONV_WIDTH), jnp.float32),
        ] + [jax.ShapeDtypeStruct((1,) + w.shape[1:], jnp.bfloat16) for w in cast],
        scratch_shapes=slot + slot + [
            pltpu.VMEM((ATTN_WIDTH, tm), jnp.float32),
            pltpu.VMEM((2, KB, GROUP * QB), jnp.float32),
            pltpu.VMEM((tm, D_MODEL), jnp.bfloat16),
            pltpu.VMEM((tm, IN_WIDTH), jnp.float32),
            pltpu.VMEM((tm, CONV_WIDTH), jnp.float32),
        ],
        compiler_params=pltpu.CompilerParams(
            dimension_semantics=("arbitrary",),
            vmem_limit_bytes=MIXER_VMEM_BYTES if n_steps > 1 else META_VMEM_BYTES),
        name="mixer",
    )(sinks, h, h, tab, ck, cv, cu, cb, pre_g, w_in, conv_w, attn_g, conv_g, w_out, post_g, *cast)


def _mlp_kernel(*refs, layer, tm, n_cast):
    h_ref, pre_g_ref, w_up_ref, w_down_ref, post_g_ref = refs[:5]
    pre_g_ref, post_g_ref = (g.at[layer:layer + 1] for g in (pre_g_ref, post_g_ref))
    cast_src = refs[5:5 + n_cast]
    o_ref = refs[5 + n_cast]
    cast_dst = refs[6 + n_cast:]
    for src, dst in zip(cast_src, cast_dst):
        dst[...] = src[...].astype(jnp.bfloat16)

    sub = min(MLP_SUB_ROWS, tm)
    subs = [slice(r, r + sub) for r in range(0, tm, sub)]
    a = [(_rms(h_ref[rows, :]) * pre_g_ref[...]).astype(jnp.bfloat16) for rows in subs]
    for a_sub, rows in zip(a, subs):
        acc = jnp.zeros((sub, D_MODEL), jnp.float32)
        for c in range(D_FF // FF_CHUNK):
            cols = slice(c * FF_CHUNK, (c + 1) * FF_CHUNK)
            up = jnp.dot(a_sub, w_up_ref[:, cols], preferred_element_type=jnp.float32)
            act = jnp.square(jnp.maximum(up, 0.0)).astype(jnp.bfloat16)
            acc = acc + jnp.dot(act, w_down_ref[cols, :], preferred_element_type=jnp.float32)
        o_ref[rows, :] = h_ref[rows, :] + _rms(acc) * post_g_ref[...]


def _mlp(h, pre_g, w_up, w_down, post_g, *, layer, w_layer, tm, cast=(), cast_layer=0):
    rows = h.shape[0]
    steps = rows // tm
    assert rows == steps * tm and tm % min(MLP_SUB_ROWS, tm) == 0, (rows, tm)
    assert all(w.shape[1] % steps == 0 for w in cast), [w.shape for w in cast]
    depth = pre_g.shape[0]
    const = lambda i: (0, 0)
    of_w_layer = lambda i: (w_layer, 0, 0)
    resident = dict(pipeline_mode=pl.Buffered(1))
    slab = lambda w: (None, w.shape[1] // steps, w.shape[2])
    outs = pl.pallas_call(
        functools.partial(_mlp_kernel, layer=layer, tm=tm, n_cast=len(cast)),
        grid=(steps,),
        in_specs=[
            pl.BlockSpec((tm, D_MODEL), lambda i: (i, 0)),
            pl.BlockSpec((depth, D_MODEL), const),
            pl.BlockSpec((None, D_MODEL, D_FF), of_w_layer, **resident),
            pl.BlockSpec((None, D_FF, D_MODEL), of_w_layer, **resident),
            pl.BlockSpec((depth, D_MODEL), const),
        ] + [pl.BlockSpec(slab(w), lambda i: (cast_layer, i, 0)) for w in cast],
        out_specs=[pl.BlockSpec((tm, D_MODEL), lambda i: (i, 0))]
        + [pl.BlockSpec(slab(w), lambda i: (0, i, 0)) for w in cast],
        out_shape=[jax.ShapeDtypeStruct((rows, D_MODEL), jnp.float32)]
        + [jax.ShapeDtypeStruct((1,) + w.shape[1:], jnp.bfloat16) for w in cast],
        compiler_params=pltpu.CompilerParams(
            dimension_semantics=("arbitrary",),
            vmem_limit_bytes=MLP_VMEM_BYTES if steps > 1 else META_VMEM_BYTES),
        name="mlp",
    )(h, pre_g, w_up, w_down, post_g, *cast)
    return outs[0], list(outs[1:])


def _rope_table(first_pos, n_pos, lead_rows=0):
    pos = first_pos + np.arange(n_pos, dtype=np.float64)
    inv_freq = np.power(ROPE_THETA, -np.arange(0, ROT_DIM, 2, dtype=np.float64) / ROT_DIM)
    half = ROT_DIM // 2
    d = np.arange(LANES) % HEAD_DIM
    ang = pos[:, None] * inv_freq[d % half][None, :]
    c = np.where(d < ROT_DIM, np.cos(ang), 1.0)
    s = np.where(d < half, -np.sin(ang), np.where(d < ROT_DIM, np.sin(ang), 0.0))
    tab = np.concatenate([c, s], axis=1).astype(np.float32)
    return jnp.asarray(np.pad(tab, ((lead_rows, 0), (0, 0))))


def kernel(x, meta_tokens, mix_pre_g, w_in, conv_w, sinks, attn_out_g, conv_out_g, w_out,
           mix_post_g, mlp_pre_g, w_up, w_down, mlp_post_g):
    batch, seq, d_model = x.shape
    depth = w_in.shape[0]
    assert d_model == D_MODEL and meta_tokens.shape == (N_META, D_MODEL), (
        x.shape, meta_tokens.shape)
    assert w_in.shape[1:] == (D_MODEL, IN_WIDTH) and w_up.shape[1:] == (D_MODEL, D_FF), (
        w_in.shape, w_up.shape)
    assert seq % ROW_TILE == 0 and (batch * seq) % MLP_TILE == 0, (batch, seq)
    bf16 = jnp.bfloat16
    f32 = jnp.float32
    lead = QB - N_META

    tab_meta = _rope_table(0, N_META, lead_rows=lead)
    tab_x = _rope_table(N_META, seq)

    hx = x.reshape(batch * seq, D_MODEL)
    hm = jnp.pad(meta_tokens.astype(x.dtype), ((lead, 0), (0, 0)))

    masked = np.full((WINDOW, 1), MASKED, np.float32)
    tail_valid = np.where(np.arange(WINDOW)[:, None] >= lead, 0.0, MASKED).astype(np.float32)
    cb_meta = jnp.asarray(np.concatenate([masked, tail_valid]))
    cb_x = jnp.asarray(np.concatenate([tail_valid, np.zeros((QB, 1), np.float32)]))
    zero_k = jnp.zeros((N_KV_HEADS, WINDOW, HEAD_DIM), bf16)
    zero_vt = jnp.zeros((KV_WIDTH, WINDOW), f32)
    zero_u = jnp.zeros((CONV_CARRY_ROWS, CONV_WIDTH), f32)

    attn_g = attn_out_g.reshape(depth, -1, 1)
    w_in_b, w_out_b = w_in[0:1].astype(bf16), w_out[0:1].astype(bf16)
    for l in range(depth):
        mixer_w = (sinks, mix_pre_g, w_in_b, conv_w, attn_g, conv_out_g, w_out_b, mix_post_g)
        hm, mk, mv, mu = _mixer(hm, tab_meta, zero_k, zero_vt, zero_u, cb_meta, *mixer_w,
                                layer=l, w_layer=0, batch=1, tm=QB)
        hx, _, _, _, w_up_b, w_down_b = _mixer(
            hx, tab_x, mk[0], mv[0], mu[0], cb_x, *mixer_w, layer=l, w_layer=0, batch=batch,
            tm=ROW_TILE, cast=(w_up, w_down), cast_layer=l)
        mlp_w = (mlp_pre_g, w_up_b, w_down_b, mlp_post_g)
        if l + 1 < depth:
            hm, _ = _mlp(hm, *mlp_w, layer=l, w_layer=0, tm=QB)
            hx, (w_in_b, w_out_b) = _mlp(hx, *mlp_w, layer=l, w_layer=0, tm=MLP_TILE,
                                         cast=(w_in, w_out), cast_layer=l + 1)
        else:
            hx, _ = _mlp(hx, *mlp_w, layer=l, w_layer=0, tm=MLP_TILE)
    return hx.reshape(batch, seq, D_MODEL)
```

```python
import functools

import jax
import jax.numpy as jnp
import numpy as np
from jax import lax
from jax.experimental import pallas as pl
from jax.experimental.pallas import tpu as pltpu

D_MODEL = 1024
N_META = 16
ATTN_WIDTH = 512
CONV_WIDTH = 512
HEAD_DIM = 64
N_Q_HEADS = 8
N_KV_HEADS = 2
GROUP = N_Q_HEADS // N_KV_HEADS
KV_WIDTH = N_KV_HEADS * HEAD_DIM
CONV_K = 3
WINDOW = 128
ROPE_THETA = 500000.0
ROT_DIM = HEAD_DIM // 4
D_FF = 4 * D_MODEL
IN_WIDTH = ATTN_WIDTH + 2 * KV_WIDTH + 3 * CONV_WIDTH
EPS = 1e-6

S_Q = ATTN_WIDTH
S_K = S_Q + KV_WIDTH
S_V = S_K + KV_WIDTH
S_B = S_V + CONV_WIDTH
S_C = S_B + CONV_WIDTH

LANES = 128
QB = WINDOW
KB = WINDOW + QB
CONV_CARRY_ROWS = 8
MASKED = -1e30
LOG2_E = 1.4426950408889634
SCORE_SCALE_LOG2 = HEAD_DIM ** -0.5 * LOG2_E
ROW_TILE = 512
FF_CHUNK = 4096
MLP_TILE = 1024
MLP_SUB_ROWS = 512
W_IN_CHUNK = 256
SCORE_LOOKAHEAD = 1
PV_DELAY = 1
V7X_VMEM_BYTES = 64 * 1024 * 1024
MIXER_VMEM_BYTES = V7X_VMEM_BYTES * 13 // 16
MLP_VMEM_BYTES = V7X_VMEM_BYTES * 11 // 16
META_VMEM_BYTES = V7X_VMEM_BYTES // 2


def _rms(x):
    return x * lax.rsqrt(jnp.mean(x * x, axis=-1, keepdims=True) + EPS)


def _rope(t, tab, low_half):
    half = ROT_DIM // 2
    partner = jnp.where(low_half,
                        pltpu.roll(t, LANES - half, axis=1),
                        pltpu.roll(t, half, axis=1))
    return t * tab[:, 0:LANES] + partner * tab[:, LANES:2 * LANES]


def _low_half(tm):
    return lax.broadcasted_iota(jnp.int32, (tm, LANES), 1) % HEAD_DIM < ROT_DIM // 2


def _post_q(chunk, proj_buf, tab_ref, q_buf, *, tm):
    tab = tab_ref[...]
    for c in range(chunk * W_IN_CHUNK // LANES, (chunk + 1) * W_IN_CHUNK // LANES):
        qc = _rope(proj_buf[:, c * LANES:(c + 1) * LANES], tab, _low_half(tm)
                   ).astype(jnp.bfloat16)
        q_buf[2 * c] = qc[:, :HEAD_DIM]
        q_buf[2 * c + 1] = qc[:, HEAD_DIM:]


def _post_kv(proj_buf, tab_ref, k_buf, vt_buf, ok_ref, ov_ref, *, tm):
    kr = (_rope(proj_buf[:, S_Q:S_K], tab_ref[...], _low_half(tm)) * SCORE_SCALE_LOG2
          ).astype(jnp.bfloat16)
    for g in range(N_KV_HEADS):
        kg = kr[:, g * HEAD_DIM:(g + 1) * HEAD_DIM]
        k_buf[g, WINDOW:WINDOW + tm, :] = kg
        ok_ref[0, g] = kg[tm - WINDOW:, :]
    vt = proj_buf[:, S_K:S_V].T
    vt_buf[:, WINDOW:WINDOW + tm] = vt
    ov_ref[0] = vt[:, tm - WINDOW:]


def _post_conv(half, proj_buf, conv_w_ref, ou_ref, u_buf, ycf_buf, *, tm):
    cols = slice(half * W_IN_CHUNK, (half + 1) * W_IN_CHUNK)
    gate = lambda start: proj_buf[:, start + half * W_IN_CHUNK:start + (half + 1) * W_IN_CHUNK]
    u = gate(S_B) * gate(S_C)
    u_buf[CONV_CARRY_ROWS:CONV_CARRY_ROWS + tm, cols] = u
    ou_ref[0, :, cols] = u[tm - CONV_CARRY_ROWS:, :]
    cw = conv_w_ref[:, cols]
    y_conv = cw[2:3, :] * u
    y_conv += cw[1:2, :] * u_buf[CONV_CARRY_ROWS - 1:CONV_CARRY_ROWS - 1 + tm, cols]
    y_conv += cw[0:1, :] * u_buf[CONV_CARRY_ROWS - 2:CONV_CARRY_ROWS - 2 + tm, cols]
    ycf_buf[:, cols] = gate(S_V) * y_conv


def _post_conv_norm(ycf_buf, conv_g_ref, yc_buf):
    yc_buf[...] = (_rms(ycf_buf[...]) * conv_g_ref[...]).astype(jnp.bfloat16)


def _scores(item, first_tile, q_buf, k_buf, bias_buf):
    j, g = divmod(item, N_KV_HEADS)
    r0 = j * QB
    bias = bias_buf[jnp.where(first_tile, 0, 1)] if j == 0 else bias_buf[1]
    kg = k_buf[g, r0:r0 + KB, :]
    qg = q_buf[g * GROUP:(g + 1) * GROUP, r0:r0 + QB, :].reshape(GROUP * QB, HEAD_DIM)
    return lax.dot_general(kg, qg, (((1,), (1,)), ((), ())),
                           preferred_element_type=jnp.float32) + bias


def _softmax(item, s, layer, sinks_ref):
    g = item % N_KV_HEADS
    sink = jnp.concatenate(
        [jnp.full((1, QB), sinks_ref[layer, g * GROUP + i] * LOG2_E, jnp.float32)
         for i in range(GROUP)], axis=1)
    m = jnp.maximum(jnp.max(s, axis=0, keepdims=True), sink)
    e = jnp.exp2(s - m)
    den = jnp.sum(e, axis=0, keepdims=True) + jnp.exp2(sink - m)
    return e.astype(jnp.bfloat16), 1.0 / den


def _pv(item, e, inv_den, vt_buf, yat_buf):
    j, g = divmod(item, N_KV_HEADS)
    r0 = j * QB
    vt = vt_buf[g * HEAD_DIM:(g + 1) * HEAD_DIM, r0:r0 + KB].astype(jnp.bfloat16)
    o = jnp.dot(vt, e, preferred_element_type=jnp.float32) * inv_den
    for i in range(GROUP):
        hd = g * GROUP + i
        yat_buf[hd * HEAD_DIM:(hd + 1) * HEAD_DIM, r0:r0 + QB] = o[:, i * QB:(i + 1) * QB]


def _mixer_out_post(h_ref, attn_g_ref, w_out_ref, post_g_ref, o_ref, yc_buf, yat_buf):
    yat = yat_buf[...]
    inv = lax.rsqrt(jnp.mean(yat * yat, axis=0, keepdims=True) + EPS)
    ya_n = (yat * inv * attn_g_ref[...]).astype(jnp.bfloat16)
    mixed = lax.dot_general(ya_n, w_out_ref[0:ATTN_WIDTH, :], (((0,), (0,)), ((), ())),
                            preferred_element_type=jnp.float32)
    mixed += jnp.dot(yc_buf[...], w_out_ref[ATTN_WIDTH:, :], preferred_element_type=jnp.float32)
    o_ref[...] = h_ref[...] + _rms(mixed) * post_g_ref[...]


N_MIXER_INPUTS = 15
N_MIXER_OUTPUTS = 4


def _mixer_kernel(*refs, layer, tm, n_tiles, n_steps, n_cast):
    (sinks_ref, h_in_ref, h_out_ref, tab_ref, ck_ref, cv_ref, cu_ref, cb_ref,
     pre_g_ref, w_in_ref, conv_w_ref, attn_g_ref, conv_g_ref, w_out_ref, post_g_ref
     ) = refs[:N_MIXER_INPUTS]
    cast_src = refs[N_MIXER_INPUTS:N_MIXER_INPUTS + n_cast]
    outs = refs[N_MIXER_INPUTS + n_cast:]
    o_ref, ok_ref, ov_ref, ou_ref = outs[:N_MIXER_OUTPUTS]
    cast_dst = outs[N_MIXER_OUTPUTS:N_MIXER_OUTPUTS + n_cast]
    (q0, k0, vt0, u0, yc0, q1, k1, vt1, u1, yc1, yat_buf, bias_buf, a_buf, proj_buf, ycf_buf
     ) = outs[N_MIXER_OUTPUTS + n_cast:]
    pre_g_ref, conv_g_ref, post_g_ref = (
        g.at[layer:layer + 1] for g in (pre_g_ref, conv_g_ref, post_g_ref))
    for src, dst in zip(cast_src, cast_dst):
        dst[...] = src[...].astype(jnp.bfloat16)
    s = pl.program_id(0)
    tile_in = jnp.minimum(s, n_steps - 1) % n_tiles
    tile_out = jnp.maximum(s - 1, 0) % n_tiles
    slots = ((q0, k0, vt0, u0, yc0), (q1, k1, vt1, u1, yc1))

    @pl.when(s == 0)
    def _():
        key_idx = lax.broadcasted_iota(jnp.int32, (KB, QB), 0)
        qry_idx = lax.broadcasted_iota(jnp.int32, (KB, QB), 1)
        band = jnp.where((key_idx > qry_idx) & (key_idx <= qry_idx + WINDOW), 0.0, MASKED)
        band = jnp.concatenate([band] * GROUP, axis=1)
        bias_buf[1] = band
        bias_buf[0] = band + cb_ref[...]
        q1[...] = jnp.zeros(q1.shape, q1.dtype)
        k1[...] = jnp.zeros(k1.shape, k1.dtype)
        vt1[...] = jnp.zeros(vt1.shape, vt1.dtype)
        yc1[...] = jnp.zeros(yc1.shape, yc1.dtype)

    def step(cur, prev):
        q_c, k_c, vt_c, u_c, yc_c = cur
        q_p, k_p, vt_p, u_p, yc_p = prev

        @pl.when(tile_in == 0)
        def _():
            k_c[:, 0:WINDOW, :] = ck_ref[...]
            vt_c[:, 0:WINDOW] = cv_ref[...]
            u_c[0:CONV_CARRY_ROWS, :] = cu_ref[...]

        @pl.when(tile_in != 0)
        def _():
            k_c[:, 0:WINDOW, :] = k_p[:, tm:tm + WINDOW, :]
            vt_c[:, 0:WINDOW] = vt_p[:, tm:tm + WINDOW]
            u_c[0:CONV_CARRY_ROWS, :] = u_p[tm:tm + CONV_CARRY_ROWS, :]

        a_buf[...] = (_rms(h_in_ref[...]) * pre_g_ref[...]).astype(jnp.bfloat16)

        def w_in_chunk(c):
            cols = slice(c * W_IN_CHUNK, (c + 1) * W_IN_CHUNK)
            proj_buf[:, cols] = jnp.dot(a_buf[...], w_in_ref[:, cols],
                                        preferred_element_type=jnp.float32)

        q_of = functools.partial(_post_q, proj_buf=proj_buf, tab_ref=tab_ref, q_buf=q_c, tm=tm)
        conv_of = functools.partial(_post_conv, proj_buf=proj_buf, conv_w_ref=conv_w_ref,
                                    ou_ref=ou_ref, u_buf=u_c, ycf_buf=ycf_buf, tm=tm)

        def conv_tail():
            conv_of(1)
            _post_conv_norm(ycf_buf, conv_g_ref, yc_c)

        chunk_plan = [
            (0, lambda: q_of(0)), (1, lambda: q_of(1)),
            (5, None), (7, None), (3, lambda: conv_of(0)),
            (6, None), (8, None), (4, conv_tail),
            (2, lambda: _post_kv(proj_buf, tab_ref, k_c, vt_c, ok_ref, ov_ref, tm=tm)),
        ]
        assert sorted(c for c, _ in chunk_plan) == list(range(IN_WIDTH // W_IN_CHUNK))

        def run_chunk(entry):
            c, post = entry
            w_in_chunk(c)
            if post is not None:
                post()

        n_items = (tm // QB) * N_KV_HEADS
        first_out = tile_out == 0
        pending = [_scores(i, first_out, q_p, k_p, bias_buf)
                   for i in range(min(SCORE_LOOKAHEAD, n_items))]
        late = []
        for item in range(n_items):
            s_cur = pending.pop(0)
            if item + SCORE_LOOKAHEAD < n_items:
                pending.append(_scores(item + SCORE_LOOKAHEAD, first_out, q_p, k_p, bias_buf))
            if len(late) == PV_DELAY:
                _pv(*late.pop(0), vt_p, yat_buf)
            if item < len(chunk_plan):
                run_chunk(chunk_plan[item])
            late.append((item,) + _softmax(item, s_cur, layer, sinks_ref))
        for entry in late:
            _pv(*entry, vt_p, yat_buf)
        for entry in chunk_plan[n_items:]:
            run_chunk(entry)
        _mixer_out_post(h_out_ref, attn_g_ref, w_out_ref, post_g_ref, o_ref, yc_p, yat_buf)

    @pl.when(s % 2 == 0)
    def _():
        step(slots[0], slots[1])

    @pl.when(s % 2 == 1)
    def _():
        step(slots[1], slots[0])


def _mixer(h, tab, ck, cv, cu, cb, sinks, pre_g, w_in, conv_w, attn_g, conv_g, w_out, post_g,
           *, layer, w_layer, batch, tm, cast=(), cast_layer=0):
    rows = h.shape[0]
    n_steps = rows // tm
    n_tiles = n_steps // batch
    assert tm % QB == 0 and rows == batch * n_tiles * tm, (rows, batch, tm)
    assert all(w.shape[1] % n_steps == 0 for w in cast), [w.shape for w in cast]
    const = lambda s: (0, 0)
    const3 = lambda s: (0, 0, 0)
    of_layer = lambda s: (layer, 0, 0)
    of_w_layer = lambda s: (w_layer, 0, 0)
    in_map = lambda s: (jnp.minimum(s, n_steps - 1), 0)
    out_map = lambda s: (jnp.maximum(s - 1, 0), 0)
    seq_of_in = lambda s: jnp.minimum(s, n_steps - 1) // n_tiles
    resident = dict(pipeline_mode=pl.Buffered(1))
    depth = pre_g.shape[0]
    slab = lambda w: (None, w.shape[1] // n_steps, w.shape[2])
    kernel = functools.partial(_mixer_kernel, layer=layer, tm=tm, n_tiles=n_tiles,
                               n_steps=n_steps, n_cast=len(cast))
    slot = [
        pltpu.VMEM((N_Q_HEADS, tm, HEAD_DIM), jnp.bfloat16),
        pltpu.VMEM((N_KV_HEADS, WINDOW + tm, HEAD_DIM), jnp.bfloat16),
        pltpu.VMEM((KV_WIDTH, WINDOW + tm), jnp.float32),
        pltpu.VMEM((CONV_CARRY_ROWS + tm, CONV_WIDTH), jnp.float32),
        pltpu.VMEM((tm, CONV_WIDTH), jnp.bfloat16),
    ]
    return pl.pallas_call(
        kernel,
        grid=(n_steps + 1,),
        in_specs=[
            pl.BlockSpec(memory_space=pltpu.SMEM),
            pl.BlockSpec((tm, D_MODEL), in_map),
            pl.BlockSpec((tm, D_MODEL), out_map),
            pl.BlockSpec((tm, 2 * LANES),
                         lambda s: (jnp.minimum(s, n_steps - 1) % n_tiles, 0)),
            pl.BlockSpec((N_KV_HEADS, WINDOW, HEAD_DIM), const3),
            pl.BlockSpec((KV_WIDTH, WINDOW), const),
            pl.BlockSpec((CONV_CARRY_ROWS, CONV_WIDTH), const),
            pl.BlockSpec((KB, 1), const),
            pl.BlockSpec((depth, D_MODEL), const),
            pl.BlockSpec((None, D_MODEL, IN_WIDTH), of_w_layer, **resident),
            pl.BlockSpec((None, CONV_K, CONV_WIDTH), of_layer),
            pl.BlockSpec((None, ATTN_WIDTH, 1), of_layer),
            pl.BlockSpec((depth, CONV_WIDTH), const),
            pl.BlockSpec((None, D_MODEL, D_MODEL), of_w_layer, **resident),
            pl.BlockSpec((depth, D_MODEL), const),
        ] + [pl.BlockSpec(slab(w), lambda s: (cast_layer, jnp.minimum(s, n_steps - 1), 0))
             for w in cast],
        out_specs=[
            pl.BlockSpec((tm, D_MODEL), out_map),
            pl.BlockSpec((1, N_KV_HEADS, WINDOW, HEAD_DIM), lambda s: (seq_of_in(s), 0, 0, 0)),
            pl.BlockSpec((1, KV_WIDTH, WINDOW), lambda s: (seq_of_in(s), 0, 0)),
            pl.BlockSpec((1, CONV_CARRY_ROWS, CONV_WIDTH), lambda s: (seq_of_in(s), 0, 0)),
        ] + [pl.BlockSpec(slab(w), lambda s: (0, jnp.minimum(s, n_steps - 1), 0)) for w in cast],
        out_shape=[
            jax.ShapeDtypeStruct((rows, D_MODEL), jnp.float32),
            jax.ShapeDtypeStruct((batch, N_KV_HEADS, WINDOW, HEAD_DIM), jnp.bfloat16),
            jax.ShapeDtypeStruct((batch, KV_WIDTH, WINDOW), jnp.float32),
            jax.ShapeDtypeStruct((batch, CONV_CARRY_ROWS, CONV_WIDTH), jnp.float32),
        ] + [jax.ShapeDtypeStruct((1,) + w.shape[1:], jnp.bfloat16) for w in cast],
        scratch_shapes=slot + slot + [
            pltpu.VMEM((ATTN_WIDTH, tm), jnp.float32),
            pltpu.VMEM((2, KB, GROUP * QB), jnp.float32),
            pltpu.VMEM((tm, D_MODEL), jnp.bfloat16),
            pltpu.VMEM((tm, IN_WIDTH), jnp.float32),
            pltpu.VMEM((tm, CONV_WIDTH), jnp.float32),
        ],
        compiler_params=pltpu.CompilerParams(
            dimension_semantics=("arbitrary",),
            vmem_limit_bytes=MIXER_VMEM_BYTES if n_steps > 1 else META_VMEM_BYTES),
        name="mixer",
    )(sinks, h, h, tab, ck, cv, cu, cb, pre_g, w_in, conv_w, attn_g, conv_g, w_out, post_g, *cast)


def _mlp_kernel(*refs, layer, tm, n_cast):
    h_ref, pre_g_ref, w_up_ref, w_down_ref, post_g_ref = refs[:5]
    pre_g_ref, post_g_ref = (g.at[layer:layer + 1] for g in (pre_g_ref, post_g_ref))
    cast_src = refs[5:5 + n_cast]
    o_ref = refs[5 + n_cast]
    cast_dst = refs[6 + n_cast:]
    for src, dst in zip(cast_src, cast_dst):
        dst[...] = src[...].astype(jnp.bfloat16)

    sub = min(MLP_SUB_ROWS, tm)
    subs = [slice(r, r + sub) for r in range(0, tm, sub)]
    a = [(_rms(h_ref[rows, :]) * pre_g_ref[...]).astype(jnp.bfloat16) for rows in subs]
    for a_sub, rows in zip(a, subs):
        acc = jnp.zeros((sub, D_MODEL), jnp.float32)
        for c in range(D_FF // FF_CHUNK):
            cols = slice(c * FF_CHUNK, (c + 1) * FF_CHUNK)
            up = jnp.dot(a_sub, w_up_ref[:, cols], preferred_element_type=jnp.float32)
            act = jnp.square(jnp.maximum(up, 0.0)).astype(jnp.bfloat16)
            acc = acc + jnp.dot(act, w_down_ref[cols, :], preferred_element_type=jnp.float32)
        o_ref[rows, :] = h_ref[rows, :] + _rms(acc) * post_g_ref[...]


def _mlp(h, pre_g, w_up, w_down, post_g, *, layer, w_layer, tm, cast=(), cast_layer=0):
    rows = h.shape[0]
    steps = rows // tm
    assert rows == steps * tm and tm % min(MLP_SUB_ROWS, tm) == 0, (rows, tm)
    assert all(w.shape[1] % steps == 0 for w in cast), [w.shape for w in cast]
    depth = pre_g.shape[0]
    const = lambda i: (0, 0)
    of_w_layer = lambda i: (w_layer, 0, 0)
    resident = dict(pipeline_mode=pl.Buffered(1))
    slab = lambda w: (None, w.shape[1] // steps, w.shape[2])
    outs = pl.pallas_call(
        functools.partial(_mlp_kernel, layer=layer, tm=tm, n_cast=len(cast)),
        grid=(steps,),
        in_specs=[
            pl.BlockSpec((tm, D_MODEL), lambda i: (i, 0)),
            pl.BlockSpec((depth, D_MODEL), const),
            pl.BlockSpec((None, D_MODEL, D_FF), of_w_layer, **resident),
            pl.BlockSpec((None, D_FF, D_MODEL), of_w_layer, **resident),
            pl.BlockSpec((depth, D_MODEL), const),
        ] + [pl.BlockSpec(slab(w), lambda i: (cast_layer, i, 0)) for w in cast],
        out_specs=[pl.BlockSpec((tm, D_MODEL), lambda i: (i, 0))]
        + [pl.BlockSpec(slab(w), lambda i: (0, i, 0)) for w in cast],
        out_shape=[jax.ShapeDtypeStruct((rows, D_MODEL), jnp.float32)]
        + [jax.ShapeDtypeStruct((1,) + w.shape[1:], jnp.bfloat16) for w in cast],
        compiler_params=pltpu.CompilerParams(
            dimension_semantics=("arbitrary",),
            vmem_limit_bytes=MLP_VMEM_BYTES if steps > 1 else META_VMEM_BYTES),
        name="mlp",
    )(h, pre_g, w_up, w_down, post_g, *cast)
    return outs[0], list(outs[1:])


def _rope_table(first_pos, n_pos, lead_rows=0):
    pos = first_pos + np.arange(n_pos, dtype=np.float64)
    inv_freq = np.power(ROPE_THETA, -np.arange(0, ROT_DIM, 2, dtype=np.float64) / ROT_DIM)
    half = ROT_DIM // 2
    d = np.arange(LANES) % HEAD_DIM
    ang = pos[:, None] * inv_freq[d % half][None, :]
    c = np.where(d < ROT_DIM, np.cos(ang), 1.0)
    s = np.where(d < half, -np.sin(ang), np.where(d < ROT_DIM, np.sin(ang), 0.0))
    tab = np.concatenate([c, s], axis=1).astype(np.float32)
    return jnp.asarray(np.pad(tab, ((lead_rows, 0), (0, 0))))


def kernel(x, meta_tokens, mix_pre_g, w_in, conv_w, sinks, attn_out_g, conv_out_g, w_out,
           mix_post_g, mlp_pre_g, w_up, w_down, mlp_post_g):
    batch, seq, d_model = x.shape
    depth = w_in.shape[0]
    assert d_model == D_MODEL and meta_tokens.shape == (N_META, D_MODEL), (
        x.shape, meta_tokens.shape)
    assert w_in.shape[1:] == (D_MODEL, IN_WIDTH) and w_up.shape[1:] == (D_MODEL, D_FF), (
        w_in.shape, w_up.shape)
    assert seq % ROW_TILE == 0 and (batch * seq) % MLP_TILE == 0, (batch, seq)
    bf16 = jnp.bfloat16
    f32 = jnp.float32
    lead = QB - N_META

    tab_meta = _rope_table(0, N_META, lead_rows=lead)
    tab_x = _rope_table(N_META, seq)

    hx = x.reshape(batch * seq, D_MODEL)
    hm = jnp.pad(meta_tokens.astype(x.dtype), ((lead, 0), (0, 0)))

    masked = np.full((WINDOW, 1), MASKED, np.float32)
    tail_valid = np.where(np.arange(WINDOW)[:, None] >= lead, 0.0, MASKED).astype(np.float32)
    cb_meta = jnp.asarray(np.concatenate([masked, tail_valid]))
    cb_x = jnp.asarray(np.concatenate([tail_valid, np.zeros((QB, 1), np.float32)]))
    zero_k = jnp.zeros((N_KV_HEADS, WINDOW, HEAD_DIM), bf16)
    zero_vt = jnp.zeros((KV_WIDTH, WINDOW), f32)
    zero_u = jnp.zeros((CONV_CARRY_ROWS, CONV_WIDTH), f32)

    attn_g = attn_out_g.reshape(depth, -1, 1)
    w_in_b, w_out_b = w_in[0:1].astype(bf16), w_out[0:1].astype(bf16)
    for l in range(depth):
        mixer_w = (sinks, mix_pre_g, w_in_b, conv_w, attn_g, conv_out_g, w_out_b, mix_post_g)
        hm, mk, mv, mu = _mixer(hm, tab_meta, zero_k, zero_vt, zero_u, cb_meta, *mixer_w,
                                layer=l, w_layer=0, batch=1, tm=QB)
        hx, _, _, _, w_up_b, w_down_b = _mixer(
            hx, tab_x, mk[0], mv[0], mu[0], cb_x, *mixer_w, layer=l, w_layer=0, batch=batch,
            tm=ROW_TILE, cast=(w_up, w_down), cast_layer=l)
        mlp_w = (mlp_pre_g, w_up_b, w_down_b, mlp_post_g)
        if l + 1 < depth:
            hm, _ = _mlp(hm, *mlp_w, layer=l, w_layer=0, tm=QB)
            hx, (w_in_b, w_out_b) = _mlp(hx, *mlp_w, layer=l, w_layer=0, tm=MLP_TILE,
                                         cast=(w_in, w_out), cast_layer=l + 1)
        else:
            hx, _ = _mlp(hx, *mlp_w, layer=l, w_layer=0, tm=MLP_TILE)
    return hx.reshape(batch, seq, D_MODEL)
```
